```python
import math
import jax, jax.numpy as jnp
from jax import lax
import numpy as np


D_MODEL = 1024
BATCH = 8
SEQ = 2048
DEPTH = 4

N_A = DEPTH // 2
N_B = DEPTH - N_A
RET_HEADS = 4
RET_QK_DIM = D_MODEL // RET_HEADS
RET_V_DIM = 2 * RET_QK_DIM
RET_CHUNK = 128
RET_ROPE_BASE = 10000.0
DIFF_HEAD_DIM = 64
DIFF_HEADS = D_MODEL // (2 * DIFF_HEAD_DIM)
DIFF_V_DIM = 2 * DIFF_HEAD_DIM
ROPE_THETA = 500000.0
ROPE_DIM = DIFF_HEAD_DIM // 4
Q_BLOCK = 128
D_FF = 4 * D_MODEL
EPS = 1e-6

kernel_name = 'yoco_retention_diffattn_sandwich_adaln'


def _rms(x, g=None):
    xf = x.astype(jnp.float32)
    y = xf * lax.rsqrt(jnp.mean(xf * xf, axis=-1, keepdims=True) + EPS)
    if g is not None:
        y = y * g.astype(jnp.float32)
    return y.astype(x.dtype)


def _rope_tables(positions, rot_dim, base, dtype):
    inv = base ** (-jnp.arange(0, rot_dim, 2, dtype=jnp.float32) / rot_dim)
    ang = positions.astype(jnp.float32)[..., None] * inv
    return jnp.cos(ang)[:, :, None, :].astype(dtype), jnp.sin(ang)[:, :, None, :].astype(dtype)


def _rope(x, cos, sin):
    r = 2 * cos.shape[-1]
    x1 = x[..., : r // 2]
    x2 = x[..., r // 2: r]
    return jnp.concatenate([x1 * cos - x2 * sin, x2 * cos + x1 * sin, x[..., r:]], axis=-1)


def _retention(h, cos, sin, w_in, w_out):
    B, S, _ = h.shape
    H, dk, dv, C = RET_HEADS, RET_QK_DIM, RET_V_DIM, RET_CHUNK
    N = S // C
    proj = h @ w_in
    q, k, v, g = jnp.split(proj, [H * dk, 2 * H * dk, 2 * H * dk + H * dv], axis=-1)
    q = _rope(q.reshape(B, S, H, dk), cos, sin)
    k = _rope(k.reshape(B, S, H, dk), cos, sin) * (dk ** -0.5)
    v = v.reshape(B, S, H, dv)
    to_chunks = lambda t: t.reshape(B, N, C, H, t.shape[-1]).transpose(1, 0, 3, 2, 4)
    qs, ks, vs = to_chunks(q), to_chunks(k), to_chunks(v)
    log_g = jnp.log1p(-jnp.exp2(-5.0 - jnp.arange(H, dtype=jnp.float32)))
    idx = jnp.arange(C, dtype=jnp.float32)
    diff = idx[:, None] - idx[None, :]
    dmask = jnp.where(diff >= 0, jnp.exp(jnp.maximum(diff, 0.0)[None] * log_g[:, None, None]), 0.0).astype(h.dtype)
    xi = jnp.exp((idx + 1.0)[None] * log_g[:, None]).astype(h.dtype)
    zeta = jnp.exp((C - 1.0 - idx)[None] * log_g[:, None]).astype(h.dtype)
    chunk_decay = jnp.exp(C * log_g).astype(h.dtype)
    intra = jnp.einsum('nbhcm,nbhme->nbhce', jnp.einsum('nbhcd,nbhmd->nbhcm', qs, ks) * dmask, vs)

    def step(R, inp):
        qc, kc, vc = inp
        cross = jnp.einsum('bhcd,bhde->bhce', qc, R) * xi[None, :, :, None]
        R = R * chunk_decay[None, :, None, None] + jnp.einsum('bhcd,bhce->bhde', kc, vc * zeta[None, :, :, None])
        return R, cross

    R0 = jnp.zeros((B, H, dk, dv), h.dtype)
    _, cross = lax.scan(step, R0, (qs, ks, vs))
    o = (intra + cross).transpose(1, 0, 3, 2, 4).reshape(B, S, H, dv)
    o = _rms(o).reshape(B, S, H * dv)
    return (jax.nn.silu(g) * o) @ w_out


def _shared_kv(x, c_act, g, ada_w, ada_b, w_kv, cos, sin):
    B, S, _ = x.shape
    shift, scale = jnp.split((c_act @ ada_w + ada_b)[:, None, :], 2, axis=-1)
    kv = (_rms(x, g) * (1 + scale) + shift) @ w_kv
    k, v = jnp.split(kv, 2, axis=-1)
    k = _rope(k.reshape(B, S, 2 * DIFF_HEADS, DIFF_HEAD_DIM), cos, sin)
    v = v.reshape(B, S, DIFF_HEADS, DIFF_V_DIM)
    return k, v


def _diff_attention(h, k, v, cos, sin, w_q, w_o, lam, subln_g, lambda_init):
    B, S, _ = h.shape
    H, d, QB = DIFF_HEADS, DIFF_HEAD_DIM, Q_BLOCK
    NB = S // QB
    q = _rope((h @ w_q).reshape(B, S, 2 * H, d), cos, sin) * (d ** -0.5)
    lf = lam.astype(jnp.float32)
    lam_full = jnp.exp(jnp.sum(lf[0] * lf[1])) - jnp.exp(jnp.sum(lf[2] * lf[3])) + lambda_init
    q_blocks = q.reshape(B, NB, QB, 2 * H, d).transpose(1, 0, 2, 3, 4)
    starts = jnp.arange(NB, dtype=jnp.int32) * QB
    key_idx = jnp.arange(S, dtype=jnp.int32)

    def block(args):
        qb, s0 = args
        s = jnp.einsum('bqhd,bkhd->bhqk', qb, k).astype(jnp.float32)
        causal = (s0 + jnp.arange(QB, dtype=jnp.int32))[:, None] >= key_idx[None, :]
        p = jax.nn.softmax(jnp.where(causal, s, -jnp.inf), axis=-1).reshape(B, H, 2, QB, S)
        w = (p[:, :, 0] - lam_full * p[:, :, 1]).astype(v.dtype)
        o = jnp.einsum('bhqk,bkhe->bqhe', w, v)
        return _rms(o, subln_g) * (1.0 - lambda_init)

    o = lax.map(block, (q_blocks, starts))
    return o.transpose(1, 0, 2, 3, 4).reshape(B, S, H * DIFF_V_DIM) @ w_o


def setup_inputs(seed: int = 0) -> dict:
    key = jax.random.key(seed)
    ks = jax.random.split(key, 18)
    D = D_MODEL
    nrm = lambda k, shape, s: jax.random.normal(k, shape, jnp.float32) * s
    x = nrm(ks[0], (BATCH, SEQ, D), 1.0)
    c = nrm(ks[1], (BATCH, D), 1.0)
    offsets = jax.random.randint(ks[2], (BATCH, 1), 0, 1024, dtype=jnp.int32)
    positions = offsets + jnp.arange(SEQ, dtype=jnp.int32)[None, :]
    return {
        'x': x,
        'c': c,
        'positions': positions,
        'norm_g': 1.0 + nrm(ks[3], (DEPTH, 4, D), 0.02),
        'ada_w': nrm(ks[4], (DEPTH, D, 6 * D), 0.5 * D ** -0.5),
        'ada_b': nrm(ks[5], (DEPTH, 6 * D), 0.01),
        'ret_w_in': nrm(ks[6], (N_A, D, 2 * RET_HEADS * RET_QK_DIM + 2 * RET_HEADS * RET_V_DIM), D ** -0.5),
        'ret_w_out': nrm(ks[7], (N_A, RET_HEADS * RET_V_DIM, D), (RET_HEADS * RET_V_DIM) ** -0.5),
        'kv_norm_g': 1.0 + nrm(ks[8], (D,), 0.02),
        'kv_ada_w': nrm(ks[9], (D, 2 * D), 0.5 * D ** -0.5),
        'kv_ada_b': nrm(ks[10], (2 * D,), 0.01),
        'kv_w': nrm(ks[11], (D, 2 * DIFF_HEADS * DIFF_HEAD_DIM + DIFF_HEADS * DIFF_V_DIM), D ** -0.5),
        'diff_w_q': nrm(ks[12], (N_B, D, 2 * DIFF_HEADS * DIFF_HEAD_DIM), D ** -0.5),
        'diff_w_o': nrm(ks[13], (N_B, DIFF_HEADS * DIFF_V_DIM, D), (DIFF_HEADS * DIFF_V_DIM) ** -0.5),
        'diff_lam': nrm(ks[14], (N_B, 4, DIFF_HEAD_DIM), 0.1),
        'diff_subln_g': 1.0 + nrm(ks[15], (N_B, DIFF_V_DIM), 0.02),
        'mlp_w1': nrm(ks[16], (DEPTH, D, D_FF), D ** -0.5),
        'mlp_w2': nrm(ks[17], (DEPTH, D_FF, D), D_FF ** -0.5),
    }


def reference(x, c, positions, norm_g, ada_w, ada_b, ret_w_in, ret_w_out, kv_norm_g, kv_ada_w, kv_ada_b,
              kv_w, diff_w_q, diff_w_o, diff_lam, diff_subln_g, mlp_w1, mlp_w2):
    c_act = jax.nn.silu(c)
    ret_cos, ret_sin = _rope_tables(positions, RET_QK_DIM, RET_ROPE_BASE, x.dtype)
    cos, sin = _rope_tables(positions, ROPE_DIM, ROPE_THETA, x.dtype)
    k_sh, v_sh = None, None
    for l in range(DEPTH):
        if l == N_A:
            k_sh, v_sh = _shared_kv(x, c_act, kv_norm_g, kv_ada_w, kv_ada_b, kv_w, cos, sin)
        mod = (c_act @ ada_w[l] + ada_b[l])[:, None, :]
        sh_a, sc_a, ga_a, sh_m, sc_m, ga_m = jnp.split(mod, 6, axis=-1)
        h = _rms(x, norm_g[l, 0]) * (1 + sc_a) + sh_a
        if l < N_A:
            y = _retention(h, ret_cos, ret_sin, ret_w_in[l], ret_w_out[l])
        else:
            j = l - N_A
            y = _diff_attention(h, k_sh, v_sh, cos, sin, diff_w_q[j], diff_w_o[j], diff_lam[j],
                                diff_subln_g[j], 0.8 - 0.6 * math.exp(-0.3 * l))
        x = x + (1 + ga_a) * _rms(y, norm_g[l, 1])
        h = _rms(x, norm_g[l, 2]) * (1 + sc_m) + sh_m
        y = jnp.square(jax.nn.relu(h @ mlp_w1[l])) @ mlp_w2[l]
        x = x + (1 + ga_m) * _rms(y, norm_g[l, 3])
    return x
```

```python
import functools
import math

import jax
import jax.numpy as jnp
from jax import lax
from jax.experimental import pallas as pl
from jax.experimental.pallas import tpu as pltpu

D_MODEL = 1024
DEPTH = 4
N_A = DEPTH // 2
RET_HEADS = 4
RET_QK_DIM = D_MODEL // RET_HEADS
RET_V_DIM = 2 * RET_QK_DIM
RET_ROPE_BASE = 10000.0
DIFF_HEAD_DIM = 64
DIFF_HEADS = D_MODEL // (2 * DIFF_HEAD_DIM)
DIFF_V_DIM = 2 * DIFF_HEAD_DIM
ROPE_THETA = 500000.0
ROPE_DIM = DIFF_HEAD_DIM // 4
D_FF = 4 * D_MODEL
EPS = 1e-6

LANES = 128
VMEM_LIMIT = 56 * 1024 * 1024

TM_PROJ = 1024
TN_PROJ = 512
TM_MLP = 1024
TF_MLP = 512
RET_CHUNK = 256
TQ = 256
TM_TAB = 2048

F32 = jnp.float32
BF16 = jnp.bfloat16


def _cparams(*sem):
    return pltpu.CompilerParams(dimension_semantics=sem, vmem_limit_bytes=VMEM_LIMIT)


def _rms(x):
    return x * lax.rsqrt(jnp.mean(x * x, axis=-1, keepdims=True) + EPS)


def _dot(a, b):
    return jnp.dot(a, b, preferred_element_type=F32)


def _dot_nt(a, b):
    return lax.dot_general(a, b, (((1,), (1,)), ((), ())), preferred_element_type=F32)


def _dot_tn(a, b):
    return lax.dot_general(a, b, (((0,), (0,)), ((), ())), preferred_element_type=F32)


def _mod_kernel(c_ref, w_ref, b_ref, o_ref):
    c = c_ref[...]
    c_act = c * (1.0 / (1.0 + jnp.exp(-c)))
    o_ref[...] = _dot(c_act.astype(BF16), w_ref[...].astype(BF16)) + b_ref[...]


def _modulation(c, w, b, tn):
    L, D, N = w.shape
    B = c.shape[0]
    return pl.pallas_call(
        _mod_kernel,
        grid=(L, N // tn),
        in_specs=[
            pl.BlockSpec((B, D), lambda l, j: (0, 0)),
            pl.BlockSpec((None, D, tn), lambda l, j: (l, 0, j)),
            pl.BlockSpec((None, 1, tn), lambda l, j: (l, 0, j)),
        ],
        out_specs=pl.BlockSpec((None, B, tn), lambda l, j: (l, 0, j)),
        out_shape=jax.ShapeDtypeStruct((L, B, N), F32),
        compiler_params=_cparams("parallel", "parallel"),
        name="adaln_mod",
    )(c, w, b.reshape(L, 1, N))


def _tables_kernel(pos_ref, invr_ref, invd_ref, rc_ref, rs_ref, dc_ref, ds1_ref, ds2_ref):
    p = pos_ref[...].astype(F32)
    ang = p * invr_ref[...]
    rc_ref[...] = jnp.cos(ang)
    rs_ref[...] = jnp.sin(ang)
    angd = p * invd_ref[...]
    jm = jnp.bitwise_and(lax.broadcasted_iota(jnp.int32, angd.shape, 1), DIFF_HEAD_DIM - 1)
    c = jnp.cos(angd)
    s = jnp.sin(angd)
    half = ROPE_DIM // 2
    dc_ref[...] = jnp.where(jm < ROPE_DIM, c, 1.0)
    ds1_ref[...] = jnp.where(jm < half, -s, 0.0)
    ds2_ref[...] = jnp.where(jm < half, 0.0, jnp.where(jm < ROPE_DIM, s, 0.0))


def _rope_tables(positions):
    T = positions.size
    inv_r = RET_ROPE_BASE ** (-jnp.arange(0, RET_QK_DIM, 2, dtype=F32) / RET_QK_DIM)
    inv_d = ROPE_THETA ** (-jnp.arange(0, ROPE_DIM, 2, dtype=F32) / ROPE_DIM)
    pat = jnp.concatenate([inv_d, inv_d, jnp.zeros((DIFF_HEAD_DIM - ROPE_DIM,), F32)])
    inv_d_lanes = jnp.tile(pat, LANES // DIFF_HEAD_DIM)
    tab = jax.ShapeDtypeStruct((T, LANES), F32)
    tspec = pl.BlockSpec((TM_TAB, LANES), lambda i: (i, 0))
    vspec = pl.BlockSpec((1, LANES), lambda i: (0, 0))
    return pl.pallas_call(
        _tables_kernel,
        grid=(T // TM_TAB,),
        in_specs=[pl.BlockSpec((TM_TAB, 1), lambda i: (i, 0)), vspec, vspec],
        out_specs=[tspec] * 5,
        out_shape=[tab] * 5,
        compiler_params=_cparams("parallel"),
        name="rope_tables",
    )(positions.reshape(T, 1), inv_r.reshape(1, LANES), inv_d_lanes.reshape(1, LANES))


def _proj_kernel(x_ref, g_ref, sc_ref, sh_ref, w_ref, *rest, mode, rope_tiles, scale_lo, scale):
    tabs, o_ref, h_ref = rest[:-2], rest[-2], rest[-1]
    j = pl.program_id(1)

    @pl.when(j == 0)
    def _():
        y = _rms(x_ref[...]) * g_ref[...]
        h_ref[...] = (y * (1.0 + sc_ref[...]) + sh_ref[...]).astype(BF16)

    acc = _dot(h_ref[...], w_ref[...])
    tn = acc.shape[1]

    @pl.when(j < rope_tiles)
    def _():
        mul = jnp.where(j >= scale_lo, scale, 1.0).astype(F32)
        if mode == "ret":
            cos = tabs[0][...]
            sin = tabs[1][...]
            for hh in range(tn // RET_QK_DIM):
                a = hh * RET_QK_DIM
                x1 = acc[:, a:a + LANES]
                x2 = acc[:, a + LANES:a + 2 * LANES]
                o_ref[:, a:a + LANES] = ((x1 * cos - x2 * sin) * mul).astype(o_ref.dtype)
                o_ref[:, a + LANES:a + 2 * LANES] = ((x2 * cos + x1 * sin) * mul).astype(o_ref.dtype)
        else:
            c = tabs[0][...]
            s1 = tabs[1][...]
            s2 = tabs[2][...]
            half = ROPE_DIM // 2
            for cc in range(tn // LANES):
                xc = acc[:, cc * LANES:(cc + 1) * LANES]
                up = pltpu.roll(xc, LANES - half, 1)
                dn = pltpu.roll(xc, half, 1)
                o_ref[:, cc * LANES:(cc + 1) * LANES] = ((xc * c + up * s1 + dn * s2) * mul).astype(o_ref.dtype)

    @pl.when(j >= rope_tiles)
    def _():
        o_ref[...] = acc.astype(o_ref.dtype)


def _proj(x, g, sc, sh, w, tabs, *, mode, rope_tiles, scale_lo, scale, seq):
    T, D = x.shape
    N = w.shape[1]
    tpb = seq // TM_PROJ
    vec = pl.BlockSpec((1, D), lambda i, j: (0, 0))
    bvec = pl.BlockSpec((None, 1, D), lambda i, j: (i // tpb, 0, 0))
    tspec = pl.BlockSpec((TM_PROJ, LANES), lambda i, j: (i, 0))
    kern = functools.partial(_proj_kernel, mode=mode, rope_tiles=rope_tiles, scale_lo=scale_lo, scale=scale)
    return pl.pallas_call(
        kern,
        grid=(T // TM_PROJ, N // TN_PROJ),
        in_specs=[
            pl.BlockSpec((TM_PROJ, D), lambda i, j: (i, 0)), vec, bvec, bvec,
            pl.BlockSpec((D, TN_PROJ), lambda i, j: (0, j)),
        ] + [tspec] * len(tabs),
        out_specs=pl.BlockSpec((TM_PROJ, TN_PROJ), lambda i, j: (i, j)),
        out_shape=jax.ShapeDtypeStruct((T, N), BF16),
        scratch_shapes=[pltpu.VMEM((TM_PROJ, D), BF16)],
        compiler_params=_cparams("parallel", "arbitrary"),
        name="proj_" + mode,
    )(x, g, sc, sh, w, *tabs)


def _mixer_epilogue(y, x_ref, ga_ref, g1_ref, g2_ref, scm_ref, shm_ref, xo_ref, h2_ref):
    xn = x_ref[...] + (1.0 + ga_ref[...]) * (_rms(y) * g1_ref[...])
    xo_ref[...] = xn
    h2_ref[...] = ((_rms(xn) * g2_ref[...]) * (1.0 + scm_ref[...]) + shm_ref[...]).astype(h2_ref.dtype)


def _ret_kernel(q_ref, k_ref, v_ref, gt_ref, x_ref, wo_ref, ga_ref, g1_ref, g2_ref, scm_ref, shm_ref,
                xo_ref, h2_ref, r_ref, dm_ref, xi_ref, ze_ref, z_ref):
    C = RET_CHUNK
    n = pl.program_id(1)
    log_g = [math.log1p(-(2.0 ** (-5 - h))) for h in range(RET_HEADS)]

    @pl.when(n == 0)
    def _():
        r_ref[...] = jnp.zeros_like(r_ref)
        diff = (lax.broadcasted_iota(jnp.int32, (C, C), 0) - lax.broadcasted_iota(jnp.int32, (C, C), 1)).astype(F32)
        idx = lax.broadcasted_iota(jnp.int32, (C, RET_V_DIM), 0).astype(F32)
        for h in range(RET_HEADS):
            dm_ref[h] = jnp.where(diff >= 0, jnp.exp(jnp.maximum(diff, 0.0) * log_g[h]), 0.0)
            xi_ref[h] = jnp.exp((idx + 1.0) * log_g[h])
            ze_ref[h] = jnp.exp((C - 1.0 - idx) * log_g[h])

    for h in range(RET_HEADS):
        qh = q_ref[:, h * RET_QK_DIM:(h + 1) * RET_QK_DIM]
        kh = k_ref[:, h * RET_QK_DIM:(h + 1) * RET_QK_DIM]
        vh = v_ref[:, h * RET_V_DIM:(h + 1) * RET_V_DIM]
        s = (_dot_nt(qh, kh) * dm_ref[h]).astype(BF16)
        intra = _dot(s, vh)
        rh = r_ref[h]
        cross = _dot(qh, rh.astype(BF16)) * xi_ref[h]
        vz = (vh.astype(F32) * ze_ref[h]).astype(BF16)
        r_ref[h] = rh * math.exp(C * log_g[h]) + _dot_tn(kh, vz)
        o = _rms(intra + cross)
        gt = gt_ref[:, h * RET_V_DIM:(h + 1) * RET_V_DIM].astype(F32)
        z_ref[:, h * RET_V_DIM:(h + 1) * RET_V_DIM] = (gt * (1.0 / (1.0 + jnp.exp(-gt))) * o).astype(BF16)

    y = _dot(z_ref[...], wo_ref[...])
    _mixer_epilogue(y, x_ref, ga_ref, g1_ref, g2_ref, scm_ref, shm_ref, xo_ref, h2_ref)


def _retention(qkvg, x, wo, ga, g1, g2, scm, shm, *, batch, seq):
    T, D = x.shape
    C = RET_CHUNK
    nch = seq // C
    HK = RET_HEADS * RET_QK_DIM
    HV = RET_HEADS * RET_V_DIM
    row = lambda b, n: b * nch + n
    vec = pl.BlockSpec((1, D), lambda b, n: (0, 0))
    bvec = pl.BlockSpec((None, 1, D), lambda b, n: (b, 0, 0))
    xspec = pl.BlockSpec((C, D), lambda b, n: (row(b, n), 0))
    return pl.pallas_call(
        _ret_kernel,
        grid=(batch, nch),
        in_specs=[
            pl.BlockSpec((C, HK), lambda b, n: (row(b, n), 0)),
            pl.BlockSpec((C, HK), lambda b, n: (row(b, n), 1)),
            pl.BlockSpec((C, HV), lambda b, n: (row(b, n), 1)),
            pl.BlockSpec((C, HV), lambda b, n: (row(b, n), 2)),
            xspec,
            pl.BlockSpec((HV, D), lambda b, n: (0, 0)),
            bvec, vec, vec, bvec, bvec,
        ],
        out_specs=[xspec, xspec],
        out_shape=[jax.ShapeDtypeStruct((T, D), F32), jax.ShapeDtypeStruct((T, D), BF16)],
        scratch_shapes=[
            pltpu.VMEM((RET_HEADS, RET_QK_DIM, RET_V_DIM), F32),
            pltpu.VMEM((RET_HEADS, C, C), F32),
            pltpu.VMEM((RET_HEADS, C, RET_V_DIM), F32),
            pltpu.VMEM((RET_HEADS, C, RET_V_DIM), F32),
            pltpu.VMEM((C, HV), BF16),
        ],
        compiler_params=_cparams("arbitrary", "arbitrary"),
        name="retention",
    )(qkvg, qkvg, qkvg, qkvg, x, wo, ga, g1, g2, scm, shm)


def _attn_kernel(lam_ref, q_ref, k_ref, v_ref, x_ref, wo_ref, sg_ref, ga_ref, g1_ref, g2_ref, scm_ref, shm_ref,
                 xo_ref, h2_ref, s1_ref, s2_ref, m1_ref, m2_ref, l1_ref, l2_ref, acc_ref, z_ref, *, lambda_init):
    qi = pl.program_id(1)
    lf = lam_ref[...]
    lam = (jnp.exp(jnp.sum(lf[0:1] * lf[1:2], axis=-1, keepdims=True))
           - jnp.exp(jnp.sum(lf[2:3] * lf[3:4], axis=-1, keepdims=True)) + lambda_init)
    lane = lax.broadcasted_iota(jnp.int32, (TQ, LANES), 1)
    causal = lax.broadcasted_iota(jnp.int32, (TQ, TQ), 0) >= lax.broadcasted_iota(jnp.int32, (TQ, TQ), 1)
    neg_inf = jnp.full((TQ, LANES), -jnp.inf, F32)

    def fold_max(m_ref, s):
        m_ref[...] = jnp.maximum(m_ref[...], jnp.maximum(s[:, :LANES], s[:, LANES:]))

    for p in range(DIFF_HEADS):
        cols = slice(p * DIFF_V_DIM, (p + 1) * DIFF_V_DIM)
        qp = q_ref[:, cols]
        zero = jnp.zeros_like(qp)
        q1 = jnp.where(lane < DIFF_HEAD_DIM, qp, zero)
        q2 = jnp.where(lane < DIFF_HEAD_DIM, zero, qp)
        m1_ref[...] = neg_inf
        m2_ref[...] = neg_inf

        def scores(kb):
            kblk = k_ref[pl.ds(pl.multiple_of(kb * TQ, TQ), TQ), cols]
            return _dot_nt(q1, kblk), _dot_nt(q2, kblk)

        def pass1(kb, carry):
            s1, s2 = scores(kb)
            s1_ref[kb] = s1
            s2_ref[kb] = s2
            fold_max(m1_ref, s1)
            fold_max(m2_ref, s2)
            return carry

        lax.fori_loop(0, qi, pass1, 0)
        s1, s2 = scores(qi)
        s1 = jnp.where(causal, s1, -jnp.inf)
        s2 = jnp.where(causal, s2, -jnp.inf)
        s1_ref[qi] = s1
        s2_ref[qi] = s2
        fold_max(m1_ref, s1)
        fold_max(m2_ref, s2)
        m1 = jnp.max(m1_ref[...], axis=-1, keepdims=True)
        m2 = jnp.max(m2_ref[...], axis=-1, keepdims=True)

        l1_ref[...] = jnp.zeros_like(l1_ref)
        l2_ref[...] = jnp.zeros_like(l2_ref)
        acc_ref[...] = jnp.zeros_like(acc_ref)

        def pass2(kb, carry):
            e1 = jnp.exp(s1_ref[kb] - m1)
            e2 = jnp.exp(s2_ref[kb] - m2)
            l1_ref[...] += e1[:, :LANES] + e1[:, LANES:]
            l2_ref[...] += e2[:, :LANES] + e2[:, LANES:]
            vblk = v_ref[pl.ds(pl.multiple_of(kb * TQ, TQ), TQ), cols]
            e = jnp.concatenate([e1.astype(BF16), e2.astype(BF16)], axis=0)
            acc_ref[...] += _dot(e, vblk)
            return carry

        lax.fori_loop(0, qi + 1, pass2, 0)
        r1 = 1.0 / jnp.sum(l1_ref[...], axis=-1, keepdims=True)
        r2 = 1.0 / jnp.sum(l2_ref[...], axis=-1, keepdims=True)
        o = acc_ref[:TQ, :] * r1 - lam * (acc_ref[TQ:, :] * r2)
        z_ref[:, cols] = ((_rms(o) * sg_ref[...]) * (1.0 - lambda_init)).astype(BF16)

    y = _dot(z_ref[...], wo_ref[...])
    _mixer_epilogue(y, x_ref, ga_ref, g1_ref, g2_ref, scm_ref, shm_ref, xo_ref, h2_ref)


def _diff_attention(q, kv, x, wo, lam, sg, ga, g1, g2, scm, shm, *, lambda_init, batch, seq):
    T, D = x.shape
    nq = seq // TQ
    vec = pl.BlockSpec((1, D), lambda b, i: (0, 0))
    bvec = pl.BlockSpec((None, 1, D), lambda b, i: (b, 0, 0))
    qspec = pl.BlockSpec((TQ, D), lambda b, i: (b * nq + i, 0))
    kern = functools.partial(_attn_kernel, lambda_init=lambda_init)
    return pl.pallas_call(
        kern,
        grid=(batch, nq),
        in_specs=[
            pl.BlockSpec(lam.shape, lambda b, i: (0, 0)),
            qspec,
            pl.BlockSpec((seq, D), lambda b, i: (b, 0)),
            pl.BlockSpec((seq, D), lambda b, i: (b, 1)),
            qspec,
            pl.BlockSpec((D, D), lambda b, i: (0, 0)),
            pl.BlockSpec((1, DIFF_V_DIM), lambda b, i: (0, 0)),
            bvec, vec, vec, bvec, bvec,
        ],
        out_specs=[qspec, qspec],
        out_shape=[jax.ShapeDtypeStruct((T, D), F32), jax.ShapeDtypeStruct((T, D), BF16)],
        scratch_shapes=[
            pltpu.VMEM((nq, TQ, TQ), F32), pltpu.VMEM((nq, TQ, TQ), F32),
            pltpu.VMEM((TQ, LANES), F32), pltpu.VMEM((TQ, LANES), F32),
            pltpu.VMEM((TQ, LANES), F32), pltpu.VMEM((TQ, LANES), F32),
            pltpu.VMEM((2 * TQ, DIFF_V_DIM), F32),
            pltpu.VMEM((TQ, D), BF16),
        ],
        compiler_params=_cparams("parallel", "arbitrary"),
        name="diff_attention",
    )(lam, q, kv, kv, x, wo, sg, ga, g1, g2, scm, shm)


def _mlp_kernel(h_ref, w1_ref, w2_ref, x_ref, ga_ref, g3_ref, xo_ref, acc_ref):
    f = pl.program_id(1)
    a = jnp.maximum(_dot(h_ref[...], w1_ref[...]), 0.0)
    part = _dot((a * a).astype(BF16), w2_ref[...])

    @pl.when(f == 0)
    def _():
        acc_ref[...] = part

    @pl.when(f > 0)
    def _():
        acc_ref[...] += part

    @pl.when(f == pl.num_programs(1) - 1)
    def _():
        xo_ref[...] = x_ref[...] + (1.0 + ga_ref[...]) * (_rms(acc_ref[...]) * g3_ref[...])


def _mlp(h, w1, w2, x, ga, g3, *, seq):
    T, D = x.shape
    F = w1.shape[1]
    tpb = seq // TM_MLP
    xspec = pl.BlockSpec((TM_MLP, D), lambda i, f: (i, 0))
    return pl.pallas_call(
        _mlp_kernel,
        grid=(T // TM_MLP, F // TF_MLP),
        in_specs=[
            xspec,
            pl.BlockSpec((D, TF_MLP), lambda i, f: (0, f)),
            pl.BlockSpec((TF_MLP, D), lambda i, f: (f, 0)),
            xspec,
            pl.BlockSpec((None, 1, D), lambda i, f: (i // tpb, 0, 0)),
            pl.BlockSpec((1, D), lambda i, f: (0, 0)),
        ],
        out_specs=xspec,
        out_shape=jax.ShapeDtypeStruct((T, D), F32),
        scratch_shapes=[pltpu.VMEM((TM_MLP, D), F32)],
        compiler_params=_cparams("parallel", "arbitrary"),
        name="mlp",
    )(h, w1, w2, x, ga, g3)


def kernel(x, c, positions, norm_g, ada_w, ada_b, ret_w_in, ret_w_out, kv_norm_g, kv_ada_w, kv_ada_b, kv_w,
           diff_w_q, diff_w_o, diff_lam, diff_subln_g, mlp_w1, mlp_w2):
    B, S, D = x.shape
    T = B * S
    xf = x.reshape(T, D)

    mod = _modulation(c, ada_w, ada_b, 1536)
    kv_mod = _modulation(c, kv_ada_w[None], kv_ada_b[None], 1024)[0]
    ret_cos, ret_sin, d_c, d_s1, d_s2 = _rope_tables(positions)

    def mvec(l, i):
        return mod[l, :, i * D:(i + 1) * D].reshape(B, 1, D)

    gvec = lambda l, i: norm_g[l, i].reshape(1, D)
    ret_scale = RET_QK_DIM ** -0.5
    diff_scale = DIFF_HEAD_DIM ** -0.5
    hk_tiles = RET_HEADS * RET_QK_DIM // TN_PROJ
    kv = None

    for l in range(DEPTH):
        sh_a, sc_a, ga_a, sh_m, sc_m, ga_m = (mvec(l, i) for i in range(6))
        if l == N_A:
            kv = _proj(xf, kv_norm_g.reshape(1, D), kv_mod[:, D:].reshape(B, 1, D), kv_mod[:, :D].reshape(B, 1, D),
                       kv_w.astype(BF16), (d_c, d_s1, d_s2), mode="diff", rope_tiles=D // TN_PROJ, scale_lo=0,
                       scale=1.0, seq=S)
        if l < N_A:
            qkvg = _proj(xf, gvec(l, 0), sc_a, sh_a, ret_w_in[l].astype(BF16), (ret_cos, ret_sin), mode="ret",
                         rope_tiles=2 * hk_tiles, scale_lo=hk_tiles, scale=ret_scale, seq=S)
            xf, h2 = _retention(qkvg, xf, ret_w_out[l].astype(BF16), ga_a, gvec(l, 1), gvec(l, 2), sc_m, sh_m,
                                batch=B, seq=S)
        else:
            j = l - N_A
            q = _proj(xf, gvec(l, 0), sc_a, sh_a, diff_w_q[j].astype(BF16), (d_c, d_s1, d_s2), mode="diff",
                      rope_tiles=D // TN_PROJ, scale_lo=0, scale=diff_scale, seq=S)
            xf, h2 = _diff_attention(q, kv, xf, diff_w_o[j].astype(BF16), diff_lam[j],
                                     diff_subln_g[j].reshape(1, DIFF_V_DIM), ga_a, gvec(l, 1), gvec(l, 2), sc_m, sh_m,
                                     lambda_init=0.8 - 0.6 * math.exp(-0.3 * l), batch=B, seq=S)
        xf = _mlp(h2, mlp_w1[l].astype(BF16), mlp_w2[l].astype(BF16), xf, ga_m, gvec(l, 3), seq=S)
    return xf.reshape(B, S, D)
```

```python
import functools
import math

import jax
import jax.numpy as jnp
from jax import lax
from jax.experimental import pallas as pl
from jax.experimental.pallas import tpu as pltpu

D_MODEL = 1024
DEPTH = 4
N_A = DEPTH // 2
RET_HEADS = 4
RET_QK_DIM = D_MODEL // RET_HEADS
RET_V_DIM = 2 * RET_QK_DIM
RET_ROPE_BASE = 10000.0
DIFF_HEAD_DIM = 64
DIFF_HEADS = D_MODEL // (2 * DIFF_HEAD_DIM)
DIFF_V_DIM = 2 * DIFF_HEAD_DIM
ROPE_THETA = 500000.0
ROPE_DIM = DIFF_HEAD_DIM // 4
D_FF = 4 * D_MODEL
EPS = 1e-6

LANES = 128
VMEM_LIMIT = 56 * 1024 * 1024

TM_PROJ = 1024
TN_PROJ = 512
TM_MLP = 512
TF_MLP = 512
RET_CHUNK = 256
TQ = 256
TM_TAB = 2048

F32 = jnp.float32
BF16 = jnp.bfloat16


def _cparams(*sem):
    return pltpu.CompilerParams(dimension_semantics=sem, vmem_limit_bytes=VMEM_LIMIT)


def _resident(shape):
    return pl.BlockSpec(shape, lambda *_: (0,) * len(shape), pipeline_mode=pl.Buffered(1))


def _rms(x):
    return x * lax.rsqrt(jnp.mean(x * x, axis=-1, keepdims=True) + EPS)


def _dot(a, b):
    return jnp.dot(a, b, preferred_element_type=F32)


def _dot_nt(a, b):
    return lax.dot_general(a, b, (((1,), (1,)), ((), ())), preferred_element_type=F32)


def _dot_tn(a, b):
    return lax.dot_general(a, b, (((0,), (0,)), ((), ())), preferred_element_type=F32)


def _mod_kernel(c_ref, w_ref, b_ref, o_ref):
    c = c_ref[...]
    c_act = c * (1.0 / (1.0 + jnp.exp(-c)))
    o_ref[...] = _dot(c_act.astype(BF16), w_ref[...].astype(BF16)) + b_ref[...]


def _modulation(c, w, b, tn):
    L, D, N = w.shape
    B = c.shape[0]
    return pl.pallas_call(
        _mod_kernel,
        grid=(L, N // tn),
        in_specs=[
            pl.BlockSpec((B, D), lambda l, j: (0, 0)),
            pl.BlockSpec((None, D, tn), lambda l, j: (l, 0, j)),
            pl.BlockSpec((None, 1, tn), lambda l, j: (l, 0, j)),
        ],
        out_specs=pl.BlockSpec((None, B, tn), lambda l, j: (l, 0, j)),
        out_shape=jax.ShapeDtypeStruct((L, B, N), F32),
        compiler_params=_cparams("parallel", "parallel"),
        name="adaln_mod",
    )(c, w, b.reshape(L, 1, N))


def _tables_kernel(pos_ref, invr_ref, invd_ref, rc_ref, rs_ref, dc_ref, ds1_ref, ds2_ref):
    p = pos_ref[...].astype(F32)
    ang = p * invr_ref[...]
    rc_ref[...] = jnp.cos(ang)
    rs_ref[...] = jnp.sin(ang)
    angd = p * invd_ref[...]
    jm = jnp.bitwise_and(lax.broadcasted_iota(jnp.int32, angd.shape, 1), DIFF_HEAD_DIM - 1)
    c = jnp.cos(angd)
    s = jnp.sin(angd)
    half = ROPE_DIM // 2
    dc_ref[...] = jnp.where(jm < ROPE_DIM, c, 1.0)
    ds1_ref[...] = jnp.where(jm < half, -s, 0.0)
    ds2_ref[...] = jnp.where(jm < half, 0.0, jnp.where(jm < ROPE_DIM, s, 0.0))


def _rope_tables(positions):
    T = positions.size
    inv_r = RET_ROPE_BASE ** (-jnp.arange(0, RET_QK_DIM, 2, dtype=F32) / RET_QK_DIM)
    inv_d = ROPE_THETA ** (-jnp.arange(0, ROPE_DIM, 2, dtype=F32) / ROPE_DIM)
    pat = jnp.concatenate([inv_d, inv_d, jnp.zeros((DIFF_HEAD_DIM - ROPE_DIM,), F32)])
    inv_d_lanes = jnp.tile(pat, LANES // DIFF_HEAD_DIM)
    tab = jax.ShapeDtypeStruct((T, LANES), F32)
    tspec = pl.BlockSpec((TM_TAB, LANES), lambda i: (i, 0))
    vspec = pl.BlockSpec((1, LANES), lambda i: (0, 0))
    return pl.pallas_call(
        _tables_kernel,
        grid=(T // TM_TAB,),
        in_specs=[pl.BlockSpec((TM_TAB, 1), lambda i: (i, 0)), vspec, vspec],
        out_specs=[tspec] * 5,
        out_shape=[tab] * 5,
        compiler_params=_cparams("parallel"),
        name="rope_tables",
    )(positions.reshape(T, 1), inv_r.reshape(1, LANES), inv_d_lanes.reshape(1, LANES))


def _proj_kernel(x_ref, g_ref, sc_ref, sh_ref, w_ref, *rest, mode, rope_tiles, scale_lo, scale):
    tabs, o_ref, h_ref = rest[:-2], rest[-2], rest[-1]
    j = pl.program_id(1)

    @pl.when(j == 0)
    def _():
        y = _rms(x_ref[...]) * g_ref[...]
        h_ref[...] = (y * (1.0 + sc_ref[...]) + sh_ref[...]).astype(BF16)

    acc = _dot(h_ref[...], w_ref[...])
    tn = acc.shape[1]

    @pl.when(j < rope_tiles)
    def _():
        mul = jnp.where(j >= scale_lo, scale, 1.0).astype(F32)
        if mode == "ret":
            cos = tabs[0][...]
            sin = tabs[1][...]
            for hh in range(tn // RET_QK_DIM):
                a = hh * RET_QK_DIM
                x1 = acc[:, a:a + LANES]
                x2 = acc[:, a + LANES:a + 2 * LANES]
                o_ref[:, a:a + LANES] = ((x1 * cos - x2 * sin) * mul).astype(o_ref.dtype)
                o_ref[:, a + LANES:a + 2 * LANES] = ((x2 * cos + x1 * sin) * mul).astype(o_ref.dtype)
        else:
            c = tabs[0][...]
            s1 = tabs[1][...]
            s2 = tabs[2][...]
            half = ROPE_DIM // 2
            for cc in range(tn // LANES):
                xc = acc[:, cc * LANES:(cc + 1) * LANES]
                up = pltpu.roll(xc, LANES - half, 1)
                dn = pltpu.roll(xc, half, 1)
                o_ref[cc] = ((xc * c + up * s1 + dn * s2) * mul).astype(o_ref.dtype)

    @pl.when(j >= rope_tiles)
    def _():
        if mode == "ret":
            o_ref[...] = acc.astype(o_ref.dtype)
        else:
            for cc in range(tn // LANES):
                o_ref[cc] = acc[:, cc * LANES:(cc + 1) * LANES].astype(o_ref.dtype)


def _proj(x, g, sc, sh, w, tabs, *, mode, rope_tiles, scale_lo, scale, batch, seq):
    T, D = x.shape
    N = w.shape[1]
    tpb = seq // TM_PROJ
    vec = pl.BlockSpec((1, D), lambda i, j: (0, 0))
    bvec = pl.BlockSpec((None, 1, D), lambda i, j: (i // tpb, 0, 0))
    tspec = pl.BlockSpec((TM_PROJ, LANES), lambda i, j: (i, 0))
    if mode == "ret":
        out_spec = pl.BlockSpec((TM_PROJ, TN_PROJ), lambda i, j: (i, j))
        out_shape = jax.ShapeDtypeStruct((T, N), BF16)
    else:
        out_spec = pl.BlockSpec((None, TN_PROJ // LANES, TM_PROJ, LANES), lambda i, j: (i // tpb, j, i % tpb, 0))
        out_shape = jax.ShapeDtypeStruct((batch, N // LANES, seq, LANES), BF16)
    kern = functools.partial(_proj_kernel, mode=mode, rope_tiles=rope_tiles, scale_lo=scale_lo, scale=scale)
    return pl.pallas_call(
        kern,
        grid=(T // TM_PROJ, N // TN_PROJ),
        in_specs=[
            pl.BlockSpec((TM_PROJ, D), lambda i, j: (i, 0)), vec, bvec, bvec,
            pl.BlockSpec((D, TN_PROJ), lambda i, j: (0, j)),
        ] + [tspec] * len(tabs),
        out_specs=out_spec,
        out_shape=out_shape,
        scratch_shapes=[pltpu.VMEM((TM_PROJ, D), BF16)],
        compiler_params=_cparams("parallel", "arbitrary"),
        name="proj_" + mode,
    )(x, g, sc, sh, w, *tabs)


def _ret_kernel(q_ref, k_ref, v_ref, gt_ref, z_ref, r_ref, dm_ref, xi_ref, ze_ref):
    C = RET_CHUNK
    n = pl.program_id(1)
    log_g = [math.log1p(-(2.0 ** (-5 - h))) for h in range(RET_HEADS)]

    @pl.when(n == 0)
    def _():
        r_ref[...] = jnp.zeros_like(r_ref)
        diff = (lax.broadcasted_iota(jnp.int32, (C, C), 0) - lax.broadcasted_iota(jnp.int32, (C, C), 1)).astype(F32)
        idx = lax.broadcasted_iota(jnp.int32, (C, RET_V_DIM), 0).astype(F32)
        for h in range(RET_HEADS):
            dm_ref[h] = jnp.where(diff >= 0, jnp.exp(jnp.maximum(diff, 0.0) * log_g[h]), 0.0)
            xi_ref[h] = jnp.exp((idx + 1.0) * log_g[h])
            ze_ref[h] = jnp.exp((C - 1.0 - idx) * log_g[h])

    for h in range(RET_HEADS):
        qh = q_ref[:, h * RET_QK_DIM:(h + 1) * RET_QK_DIM]
        kh = k_ref[:, h * RET_QK_DIM:(h + 1) * RET_QK_DIM]
        vh = v_ref[:, h * RET_V_DIM:(h + 1) * RET_V_DIM]
        s = (_dot_nt(qh, kh) * dm_ref[h]).astype(BF16)
        intra = _dot(s, vh)
        rh = r_ref[h]
        cross = _dot(qh, rh.astype(BF16)) * xi_ref[h]
        vz = (vh.astype(F32) * ze_ref[h]).astype(BF16)
        r_ref[h] = rh * math.exp(C * log_g[h]) + _dot_tn(kh, vz)
        o = _rms(intra + cross)
        gt = gt_ref[:, h * RET_V_DIM:(h + 1) * RET_V_DIM].astype(F32)
        z_ref[:, h * RET_V_DIM:(h + 1) * RET_V_DIM] = (gt * (1.0 / (1.0 + jnp.exp(-gt))) * o).astype(BF16)


def _retention(qkvg, *, batch, seq):
    T = qkvg.shape[0]
    C = RET_CHUNK
    nch = seq // C
    HK = RET_HEADS * RET_QK_DIM
    HV = RET_HEADS * RET_V_DIM
    row = lambda b, n: b * nch + n
    return pl.pallas_call(
        _ret_kernel,
        grid=(batch, nch),
        in_specs=[
            pl.BlockSpec((C, HK), lambda b, n: (row(b, n), 0)),
            pl.BlockSpec((C, HK), lambda b, n: (row(b, n), 1)),
            pl.BlockSpec((C, HV), lambda b, n: (row(b, n), 1)),
            pl.BlockSpec((C, HV), lambda b, n: (row(b, n), 2)),
        ],
        out_specs=pl.BlockSpec((C, HV), lambda b, n: (row(b, n), 0)),
        out_shape=jax.ShapeDtypeStruct((T, HV), BF16),
        scratch_shapes=[
            pltpu.VMEM((RET_HEADS, RET_QK_DIM, RET_V_DIM), F32),
            pltpu.VMEM((RET_HEADS, C, C), F32),
            pltpu.VMEM((RET_HEADS, C, RET_V_DIM), F32),
            pltpu.VMEM((RET_HEADS, C, RET_V_DIM), F32),
        ],
        compiler_params=_cparams("arbitrary", "arbitrary"),
        name="retention",
    )(qkvg, qkvg, qkvg, qkvg)


def _attn_kernel(lam_ref, q_ref, k_ref, v_ref, sg_ref, z_ref, vaug_ref, *, lambda_init):
    S = q_ref.shape[0]
    lf = lam_ref[...]
    lam = (jnp.exp(jnp.sum(lf[0:1] * lf[1:2], axis=-1, keepdims=True))
           - jnp.exp(jnp.sum(lf[2:3] * lf[3:4], axis=-1, keepdims=True)) + lambda_init)
    lane = lax.broadcasted_iota(jnp.int32, (TQ, LANES), 1)
    causal = lax.broadcasted_iota(jnp.int32, (TQ, TQ), 0) >= lax.broadcasted_iota(jnp.int32, (TQ, TQ), 1)
    vaug_ref[:, :LANES] = v_ref[...]
    vaug_ref[:, LANES:] = jnp.ones((S, LANES), BF16)

    def softmax_numerators(qm, keys, n_keys):
        s = _dot_nt(qm, keys)
        sd = jnp.where(causal, s[:, n_keys - TQ:], -jnp.inf)
        s = sd if n_keys == TQ else jnp.concatenate([s[:, :n_keys - TQ], sd], axis=1)
        return jnp.exp(s - jnp.max(s, axis=-1, keepdims=True)).astype(BF16)

    for i in range(S // TQ):
        n_keys = (i + 1) * TQ
        qp = q_ref[i * TQ:(i + 1) * TQ, :]
        zero = jnp.zeros_like(qp)
        keys = k_ref[:n_keys, :]
        e1 = softmax_numerators(jnp.where(lane < DIFF_HEAD_DIM, qp, zero), keys, n_keys)
        e2 = softmax_numerators(jnp.where(lane < DIFF_HEAD_DIM, zero, qp), keys, n_keys)
        res = _dot(jnp.concatenate([e1, e2], axis=0), vaug_ref[:n_keys, :])
        o = (res[:TQ, :LANES] * (1.0 / res[:TQ, LANES:])
             - lam * (res[TQ:, :LANES] * (1.0 / res[TQ:, LANES:])))
        z_ref[i * TQ:(i + 1) * TQ, :] = ((_rms(o) * sg_ref[...]) * (1.0 - lambda_init)).astype(BF16)


def _diff_attention(q, kv, lam, sg, *, lambda_init):
    B, H, S, _ = q.shape
    slab = lambda off: pl.BlockSpec((None, None, S, LANES), lambda b, p: (b, p + off, 0, 0))
    kern = functools.partial(_attn_kernel, lambda_init=lambda_init)
    return pl.pallas_call(
        kern,
        grid=(B, H),
        in_specs=[
            pl.BlockSpec(lam.shape, lambda b, p: (0, 0)),
            slab(0), slab(0), slab(H),
            pl.BlockSpec((1, DIFF_V_DIM), lambda b, p: (0, 0)),
        ],
        out_specs=pl.BlockSpec((S, LANES), lambda b, p: (b, p)),
        out_shape=jax.ShapeDtypeStruct((B * S, H * LANES), BF16),
        scratch_shapes=[pltpu.VMEM((S, 2 * LANES), BF16)],
        compiler_params=_cparams("parallel", "parallel"),
        name="diff_attention",
    )(lam, q, kv, kv, sg)


def _post_mlp_kernel(z_ref, wo_ref, x_ref, gaa_ref, g1_ref, g2_ref, scm_ref, shm_ref, w1_ref, w2_ref, gam_ref, g3_ref,
                     xo_ref, a_ref):
    y = _dot(z_ref[...], wo_ref[...])
    xn = x_ref[...] + (1.0 + gaa_ref[...]) * (_rms(y) * g1_ref[...])
    xo_ref[...] = xn
    h = ((_rms(xn) * g2_ref[...]) * (1.0 + scm_ref[...]) + shm_ref[...]).astype(BF16)
    for f in range(a_ref.shape[1] // TF_MLP):
        cols = slice(f * TF_MLP, (f + 1) * TF_MLP)
        a = jnp.maximum(_dot(h, w1_ref[:, cols]), 0.0)
        a_ref[:, cols] = (a * a).astype(BF16)
    y2 = _dot(a_ref[...], w2_ref[...])
    xo_ref[...] = xo_ref[...] + (1.0 + gam_ref[...]) * (_rms(y2) * g3_ref[...])


def _post_mlp(z, wo, x, gaa, g1, g2, scm, shm, w1, w2, gam, g3, *, seq):
    T, D = x.shape
    KZ = z.shape[1]
    F = w1.shape[1]
    tpb = seq // TM_MLP
    xspec = pl.BlockSpec((TM_MLP, D), lambda i: (i, 0))
    vec = pl.BlockSpec((1, D), lambda i: (0, 0))
    bvec = pl.BlockSpec((None, 1, D), lambda i: (i // tpb, 0, 0))
    return pl.pallas_call(
        _post_mlp_kernel,
        grid=(T // TM_MLP,),
        in_specs=[
            pl.BlockSpec((TM_MLP, KZ), lambda i: (i, 0)), _resident((KZ, D)), xspec,
            bvec, vec, vec, bvec, bvec,
            _resident((D, F)), _resident((F, D)), bvec, vec,
        ],
        out_specs=xspec,
        out_shape=jax.ShapeDtypeStruct((T, D), F32),
        scratch_shapes=[pltpu.VMEM((TM_MLP, F), BF16)],
        compiler_params=_cparams("parallel"),
        name="post_mlp",
    )(z, wo, x, gaa, g1, g2, scm, shm, w1, w2, gam, g3)


def kernel(x, c, positions, norm_g, ada_w, ada_b, ret_w_in, ret_w_out, kv_norm_g, kv_ada_w, kv_ada_b, kv_w,
           diff_w_q, diff_w_o, diff_lam, diff_subln_g, mlp_w1, mlp_w2):
    B, S, D = x.shape
    T = B * S
    xf = x.reshape(T, D)

    mod = _modulation(c, ada_w, ada_b, 1536)
    kv_mod = _modulation(c, kv_ada_w[None], kv_ada_b[None], 1024)[0]
    ret_cos, ret_sin, d_c, d_s1, d_s2 = _rope_tables(positions)

    def mvec(l, i):
        return mod[l, :, i * D:(i + 1) * D].reshape(B, 1, D)

    gvec = lambda l, i: norm_g[l, i].reshape(1, D)
    ret_scale = RET_QK_DIM ** -0.5
    diff_scale = DIFF_HEAD_DIM ** -0.5
    hk_tiles = RET_HEADS * RET_QK_DIM // TN_PROJ
    kv = None

    for l in range(DEPTH):
        sh_a, sc_a, ga_a, sh_m, sc_m, ga_m = (mvec(l, i) for i in range(6))
        if l == N_A:
            kv = _proj(xf, kv_norm_g.reshape(1, D), kv_mod[:, D:].reshape(B, 1, D), kv_mod[:, :D].reshape(B, 1, D),
                       kv_w.astype(BF16), (d_c, d_s1, d_s2), mode="diff", rope_tiles=D // TN_PROJ, scale_lo=0,
                       scale=1.0, batch=B, seq=S)
        if l < N_A:
            qkvg = _proj(xf, gvec(l, 0), sc_a, sh_a, ret_w_in[l].astype(BF16), (ret_cos, ret_sin), mode="ret",
                         rope_tiles=2 * hk_tiles, scale_lo=hk_tiles, scale=ret_scale, batch=B, seq=S)
            z = _retention(qkvg, batch=B, seq=S)
            wo = ret_w_out[l]
        else:
            j = l - N_A
            q = _proj(xf, gvec(l, 0), sc_a, sh_a, diff_w_q[j].astype(BF16), (d_c, d_s1, d_s2), mode="diff",
                      rope_tiles=D // TN_PROJ, scale_lo=0, scale=diff_scale, batch=B, seq=S)
            z = _diff_attention(q, kv, diff_lam[j], diff_subln_g[j].reshape(1, DIFF_V_DIM),
                                lambda_init=0.8 - 0.6 * math.exp(-0.3 * l))
            wo = diff_w_o[j]
        xf = _post_mlp(z, wo.astype(BF16), xf, ga_a, gvec(l, 1), gvec(l, 2), sc_m, sh_m,
                       mlp_w1[l].astype(BF16), mlp_w2[l].astype(BF16), ga_m, gvec(l, 3), seq=S)
    return xf.reshape(B, S, D)
```

```python
import functools
import math

import jax
import jax.numpy as jnp
from jax import lax
from jax.experimental import pallas as pl
from jax.experimental.pallas import tpu as pltpu

D_MODEL = 1024
DEPTH = 4
N_A = DEPTH // 2
RET_HEADS = 4
RET_QK_DIM = D_MODEL // RET_HEADS
RET_V_DIM = 2 * RET_QK_DIM
RET_ROPE_BASE = 10000.0
DIFF_HEAD_DIM = 64
DIFF_HEADS = D_MODEL // (2 * DIFF_HEAD_DIM)
DIFF_V_DIM = 2 * DIFF_HEAD_DIM
ROPE_THETA = 500000.0
ROPE_DIM = DIFF_HEAD_DIM // 4
D_FF = 4 * D_MODEL
EPS = 1e-6

LANES = 128
VMEM_LIMIT = 56 * 1024 * 1024

TM_PROJ = 1024
TN_PROJ = 1024
TM_MLP = 512
TF_MLP = 512
RET_CHUNK = 256
TQ = 256
TM_TAB = 2048

F32 = jnp.float32
BF16 = jnp.bfloat16


def _cparams(*sem):
    return pltpu.CompilerParams(dimension_semantics=sem, vmem_limit_bytes=VMEM_LIMIT)


def _resident(stack, layer):
    return pl.BlockSpec((None,) + stack.shape[1:], lambda *_: (layer, 0, 0), pipeline_mode=pl.Buffered(1))


def _rms(x):
    return x * lax.rsqrt(jnp.mean(x * x, axis=-1, keepdims=True) + EPS)


def _dot(a, b):
    return jnp.dot(a, b, preferred_element_type=F32)


def _dot_nt(a, b):
    return lax.dot_general(a, b, (((1,), (1,)), ((), ())), preferred_element_type=F32)


def _dot_tn(a, b):
    return lax.dot_general(a, b, (((0,), (0,)), ((), ())), preferred_element_type=F32)


def _mod_kernel(c_ref, w_ref, b_ref, o_ref):
    c = c_ref[...]
    c_act = c * (1.0 / (1.0 + jnp.exp(-c)))
    o_ref[...] = _dot(c_act.astype(BF16), w_ref[...].astype(BF16)) + b_ref[...]


def _modulation(c, w, b, tn):
    L, D, N = w.shape
    B = c.shape[0]
    return pl.pallas_call(
        _mod_kernel,
        grid=(L, N // tn),
        in_specs=[
            pl.BlockSpec((B, D), lambda l, j: (0, 0)),
            pl.BlockSpec((None, D, tn), lambda l, j: (l, 0, j)),
            pl.BlockSpec((None, 1, tn), lambda l, j: (l, 0, j)),
        ],
        out_specs=pl.BlockSpec((None, B, tn), lambda l, j: (l, 0, j)),
        out_shape=jax.ShapeDtypeStruct((L, B, N), F32),
        compiler_params=_cparams("parallel", "parallel"),
        name="adaln_mod",
    )(c, w, b.reshape(L, 1, N))


def _tables_kernel(pos_ref, invr_ref, invd_ref, rc_ref, rs_ref, dc_ref, ds_ref):
    p = pos_ref[...].astype(F32)
    ang = p * invr_ref[...]
    rc_ref[...] = jnp.cos(ang)
    rs_ref[...] = jnp.sin(ang)
    angd = p * invd_ref[...]
    rotary = jnp.bitwise_and(lax.broadcasted_iota(jnp.int32, angd.shape, 1), DIFF_HEAD_DIM - 1) < ROPE_DIM
    dc_ref[...] = jnp.where(rotary, jnp.cos(angd), 1.0)
    ds_ref[...] = jnp.where(rotary, jnp.sin(angd), 0.0)


def _rope_tables(positions):
    T = positions.size
    inv_r = RET_ROPE_BASE ** (-jnp.arange(0, RET_QK_DIM, 2, dtype=F32) / RET_QK_DIM)
    inv_d = ROPE_THETA ** (-jnp.arange(0, ROPE_DIM, 2, dtype=F32) / ROPE_DIM)
    pat = jnp.concatenate([inv_d, inv_d, jnp.zeros((DIFF_HEAD_DIM - ROPE_DIM,), F32)])
    inv_d_lanes = jnp.tile(pat, LANES // DIFF_HEAD_DIM)
    tab = jax.ShapeDtypeStruct((T, LANES), F32)
    tspec = pl.BlockSpec((TM_TAB, LANES), lambda i: (i, 0))
    vspec = pl.BlockSpec((1, LANES), lambda i: (0, 0))
    return pl.pallas_call(
        _tables_kernel,
        grid=(T // TM_TAB,),
        in_specs=[pl.BlockSpec((TM_TAB, 1), lambda i: (i, 0)), vspec, vspec],
        out_specs=[tspec] * 4,
        out_shape=[tab] * 4,
        compiler_params=_cparams("parallel"),
        name="rope_tables",
    )(positions.reshape(T, 1), inv_r.reshape(1, LANES), inv_d_lanes.reshape(1, LANES))


def _rotate_half_matrix():
    n = 2 * LANES
    half = ROPE_DIM // 2
    row = lax.broadcasted_iota(jnp.int32, (n, n), 0)
    col = lax.broadcasted_iota(jnp.int32, (n, n), 1)
    cm = jnp.bitwise_and(col, DIFF_HEAD_DIM - 1)
    return jnp.where((cm < half) & (row == col + half), -1.0,
                     jnp.where((cm >= half) & (cm < ROPE_DIM) & (row == col - half), 1.0, 0.0)).astype(BF16)


def _proj_kernel(x_ref, g_ref, sc_ref, sh_ref, w_ref, c_ref, s_ref, rot_ref, o_ref, h_ref, *, rope_tiles, scale):
    j = pl.program_id(1)

    @pl.when(j == 0)
    def _():
        y = _rms(x_ref[...]) * g_ref[...]
        h_ref[...] = (y * (1.0 + sc_ref[...]) + sh_ref[...]).astype(BF16)

    acc = _dot(h_ref[...], w_ref[...])
    tn = acc.shape[1]

    @pl.when(j < rope_tiles)
    def _():
        c = c_ref[...]
        s = s_ref[...]
        for cc in range(tn // (2 * LANES)):
            xc = acc[:, cc * 2 * LANES:(cc + 1) * 2 * LANES]
            partner = _dot(xc.astype(BF16), rot_ref[...])
            for u in range(2):
                lanes = slice(u * LANES, (u + 1) * LANES)
                o_ref[2 * cc + u] = ((xc[:, lanes] * c + partner[:, lanes] * s) * scale).astype(o_ref.dtype)

    @pl.when(j >= rope_tiles)
    def _():
        for cc in range(tn // LANES):
            o_ref[cc] = acc[:, cc * LANES:(cc + 1) * LANES].astype(o_ref.dtype)


def _proj(x, g, sc, sh, w, c_tab, s_tab, *, rope_tiles, scale, batch, seq):
    T, D = x.shape
    N = w.shape[1]
    tpb = seq // TM_PROJ
    vec = pl.BlockSpec((1, D), lambda i, j: (0, 0))
    bvec = pl.BlockSpec((None, 1, D), lambda i, j: (i // tpb, 0, 0))
    tspec = pl.BlockSpec((TM_PROJ, LANES), lambda i, j: (i, 0))
    kern = functools.partial(_proj_kernel, rope_tiles=rope_tiles, scale=scale)
    return pl.pallas_call(
        kern,
        grid=(T // TM_PROJ, N // TN_PROJ),
        in_specs=[
            pl.BlockSpec((TM_PROJ, D), lambda i, j: (i, 0)), vec, bvec, bvec,
            pl.BlockSpec((D, TN_PROJ), lambda i, j: (0, j)), tspec, tspec,
            pl.BlockSpec((2 * LANES, 2 * LANES), lambda i, j: (0, 0)),
        ],
        out_specs=pl.BlockSpec((None, TN_PROJ // LANES, TM_PROJ, LANES), lambda i, j: (i // tpb, j, i % tpb, 0)),
        out_shape=jax.ShapeDtypeStruct((batch, N // LANES, seq, LANES), BF16),
        scratch_shapes=[pltpu.VMEM((TM_PROJ, D), BF16)],
        compiler_params=_cparams("parallel", "arbitrary"),
        name="proj_heads",
    )(x, g, sc, sh, w, c_tab, s_tab, _rotate_half_matrix())


def _ret_kernel(x_ref, g_ref, sc_ref, sh_ref, cos_ref, sin_ref, w_ref, z_ref, r_ref, dm_ref, xi_ref, ze_ref):
    C = RET_CHUNK
    HK = RET_HEADS * RET_QK_DIM
    HV = RET_HEADS * RET_V_DIM
    n = pl.program_id(1)
    log_g = [math.log1p(-(2.0 ** (-5 - h))) for h in range(RET_HEADS)]

    @pl.when(n == 0)
    def _():
        r_ref[...] = jnp.zeros_like(r_ref)
        diff = (lax.broadcasted_iota(jnp.int32, (C, C), 0) - lax.broadcasted_iota(jnp.int32, (C, C), 1)).astype(F32)
        idx = lax.broadcasted_iota(jnp.int32, (C, RET_V_DIM), 0).astype(F32)
        for h in range(RET_HEADS):
            dm_ref[h] = jnp.where(diff >= 0, jnp.exp(jnp.maximum(diff, 0.0) * log_g[h]), 0.0)
            xi_ref[h] = jnp.exp((idx + 1.0) * log_g[h])
            ze_ref[h] = jnp.exp((C - 1.0 - idx) * log_g[h])

    hin = ((_rms(x_ref[...]) * g_ref[...]) * (1.0 + sc_ref[...]) + sh_ref[...]).astype(BF16)
    cos = cos_ref[...]
    sin = sin_ref[...]

    def roped(col, mul):
        t = _dot(hin, w_ref[:, col:col + RET_QK_DIM])
        x1, x2 = t[:, :LANES], t[:, LANES:]
        return jnp.concatenate([(x1 * cos - x2 * sin) * mul, (x2 * cos + x1 * sin) * mul], axis=1).astype(BF16)

    for h in range(RET_HEADS):
        qh = roped(h * RET_QK_DIM, 1.0)
        kh = roped(HK + h * RET_QK_DIM, RET_QK_DIM ** -0.5)
        vh = _dot(hin, w_ref[:, 2 * HK + h * RET_V_DIM:2 * HK + (h + 1) * RET_V_DIM])
        gt = _dot(hin, w_ref[:, 2 * HK + HV + h * RET_V_DIM:2 * HK + HV + (h + 1) * RET_V_DIM])
        s = (_dot_nt(qh, kh) * dm_ref[h]).astype(BF16)
        intra = _dot(s, vh.astype(BF16))
        rh = r_ref[h]
        cross = _dot(qh, rh.astype(BF16)) * xi_ref[h]
        r_ref[h] = rh * math.exp(C * log_g[h]) + _dot_tn(kh, (vh * ze_ref[h]).astype(BF16))
        o = _rms(intra + cross)
        z_ref[:, h * RET_V_DIM:(h + 1) * RET_V_DIM] = (gt * (1.0 / (1.0 + jnp.exp(-gt))) * o).astype(BF16)


def _retention_layer(x, g, sc, sh, cos, sin, w_in, l, *, batch, seq):
    T, D = x.shape
    C = RET_CHUNK
    nch = seq // C
    HV = RET_HEADS * RET_V_DIM
    N = w_in.shape[2]
    row = lambda b, n: (b * nch + n, 0)
    vec = pl.BlockSpec((1, D), lambda b, n: (0, 0))
    bvec = pl.BlockSpec((None, 1, D), lambda b, n: (b, 0, 0))
    tspec = pl.BlockSpec((C, LANES), row)
    return pl.pallas_call(
        _ret_kernel,
        grid=(batch, nch),
        in_specs=[
            pl.BlockSpec((C, D), row), vec, bvec, bvec, tspec, tspec,
            pl.BlockSpec((None, D, N), lambda b, n: (l, 0, 0), pipeline_mode=pl.Buffered(1)),
        ],
        out_specs=pl.BlockSpec((C, HV), row),
        out_shape=jax.ShapeDtypeStruct((T, HV), BF16),
        scratch_shapes=[
            pltpu.VMEM((RET_HEADS, RET_QK_DIM, RET_V_DIM), F32),
            pltpu.VMEM((RET_HEADS, C, C), F32),
            pltpu.VMEM((RET_HEADS, C, RET_V_DIM), F32),
            pltpu.VMEM((RET_HEADS, C, RET_V_DIM), F32),
        ],
        compiler_params=_cparams("arbitrary", "arbitrary"),
        name="retention",
    )(x, g, sc, sh, cos, sin, w_in)


def _attn_kernel(lam_ref, q_ref, k_ref, v_ref, sg_ref, z_ref, vaug_ref, *, lambda_init):
    S = q_ref.shape[0]
    lf = lam_ref[...]
    lam = (jnp.exp(jnp.sum(lf[0:1] * lf[1:2], axis=-1, keepdims=True))
           - jnp.exp(jnp.sum(lf[2:3] * lf[3:4], axis=-1, keepdims=True)) + lambda_init)
    lane = lax.broadcasted_iota(jnp.int32, (TQ, LANES), 1)
    causal = lax.broadcasted_iota(jnp.int32, (TQ, TQ), 0) >= lax.broadcasted_iota(jnp.int32, (TQ, TQ), 1)
    vaug_ref[:, :LANES] = v_ref[...]
    vaug_ref[:, LANES:] = jnp.ones((S, LANES), BF16)

    def softmax_numerators(qm, keys, n_keys):
        s = _dot_nt(qm, keys)
        sd = jnp.where(causal, s[:, n_keys - TQ:], -jnp.inf)
        s = sd if n_keys == TQ else jnp.concatenate([s[:, :n_keys - TQ], sd], axis=1)
        return jnp.exp(s - jnp.max(s, axis=-1, keepdims=True)).astype(BF16)

    for i in range(S // TQ):
        n_keys = (i + 1) * TQ
        qp = q_ref[i * TQ:(i + 1) * TQ, :]
        zero = jnp.zeros_like(qp)
        keys = k_ref[:n_keys, :]
        e1 = softmax_numerators(jnp.where(lane < DIFF_HEAD_DIM, qp, zero), keys, n_keys)
        e2 = softmax_numerators(jnp.where(lane < DIFF_HEAD_DIM, zero, qp), keys, n_keys)
        res = _dot(jnp.concatenate([e1, e2], axis=0), vaug_ref[:n_keys, :])
        o = (res[:TQ, :LANES] * (1.0 / res[:TQ, LANES:])
             - lam * (res[TQ:, :LANES] * (1.0 / res[TQ:, LANES:])))
        z_ref[i * TQ:(i + 1) * TQ, :] = ((_rms(o) * sg_ref[...]) * (1.0 - lambda_init)).astype(BF16)


def _diff_attention(q, kv, lam, sg, *, lambda_init):
    B, H, S, _ = q.shape
    slab = lambda off: pl.BlockSpec((None, None, S, LANES), lambda b, p: (b, p + off, 0, 0))
    kern = functools.partial(_attn_kernel, lambda_init=lambda_init)
    return pl.pallas_call(
        kern,
        grid=(B, H),
        in_specs=[
            pl.BlockSpec(lam.shape, lambda b, p: (0, 0)),
            slab(0), slab(0), slab(H),
            pl.BlockSpec((1, DIFF_V_DIM), lambda b, p: (0, 0)),
        ],
        out_specs=pl.BlockSpec((S, LANES), lambda b, p: (b, p)),
        out_shape=jax.ShapeDtypeStruct((B * S, H * LANES), BF16),
        scratch_shapes=[pltpu.VMEM((S, 2 * LANES), BF16)],
        compiler_params=_cparams("parallel", "parallel"),
        name="diff_attention",
    )(lam, q, kv, kv, sg)


def _post_mlp_kernel(z_ref, wo_ref, x_ref, gaa_ref, g1_ref, g2_ref, scm_ref, shm_ref, w1_ref, w2_ref, gam_ref, g3_ref,
                     xo_ref, a_ref):
    y = _dot(z_ref[...], wo_ref[...])
    xn = x_ref[...] + (1.0 + gaa_ref[...]) * (_rms(y) * g1_ref[...])
    xo_ref[...] = xn
    h = ((_rms(xn) * g2_ref[...]) * (1.0 + scm_ref[...]) + shm_ref[...]).astype(BF16)
    for f in range(a_ref.shape[1] // TF_MLP):
        cols = slice(f * TF_MLP, (f + 1) * TF_MLP)
        a = jnp.maximum(_dot(h, w1_ref[:, cols]), 0.0)
        a_ref[:, cols] = (a * a).astype(BF16)
    y2 = _dot(a_ref[...], w2_ref[...])
    xo_ref[...] = xo_ref[...] + (1.0 + gam_ref[...]) * (_rms(y2) * g3_ref[...])


def _post_mlp(z, wo, lo, x, gaa, g1, g2, scm, shm, w1, w2, l, gam, g3, *, seq):
    T, D = x.shape
    KZ = z.shape[1]
    F = w1.shape[2]
    tpb = seq // TM_MLP
    xspec = pl.BlockSpec((TM_MLP, D), lambda i: (i, 0))
    vec = pl.BlockSpec((1, D), lambda i: (0, 0))
    bvec = pl.BlockSpec((None, 1, D), lambda i: (i // tpb, 0, 0))
    return pl.pallas_call(
        _post_mlp_kernel,
        grid=(T // TM_MLP,),
        in_specs=[
            pl.BlockSpec((TM_MLP, KZ), lambda i: (i, 0)), _resident(wo, lo), xspec,
            bvec, vec, vec, bvec, bvec,
            _resident(w1, l), _resident(w2, l), bvec, vec,
        ],
        out_specs=xspec,
        out_shape=jax.ShapeDtypeStruct((T, D), F32),
        scratch_shapes=[pltpu.VMEM((TM_MLP, F), BF16)],
        compiler_params=_cparams("parallel"),
        name="post_mlp",
    )(z, wo, x, gaa, g1, g2, scm, shm, w1, w2, gam, g3)


def kernel(x, c, positions, norm_g, ada_w, ada_b, ret_w_in, ret_w_out, kv_norm_g, kv_ada_w, kv_ada_b, kv_w,
           diff_w_q, diff_w_o, diff_lam, diff_subln_g, mlp_w1, mlp_w2):
    B, S, D = x.shape
    T = B * S
    xf = x.reshape(T, D)

    mod = _modulation(c, ada_w, ada_b, 1536)
    kv_mod = _modulation(c, kv_ada_w[None], kv_ada_b[None], 1024)[0]
    ret_cos, ret_sin, d_cos, d_sin = _rope_tables(positions)

    def mvec(l, i):
        return mod[l, :, i * D:(i + 1) * D].reshape(B, 1, D)

    gvec = lambda l, i: norm_g[l, i].reshape(1, D)
    kv = None
    ret_w_in, ret_w_out, diff_w_o, mlp_w1, mlp_w2 = (
        w.astype(BF16) for w in (ret_w_in, ret_w_out, diff_w_o, mlp_w1, mlp_w2))

    for l in range(DEPTH):
        sh_a, sc_a, ga_a, sh_m, sc_m, ga_m = (mvec(l, i) for i in range(6))
        if l == N_A:
            kv = _proj(xf, kv_norm_g.reshape(1, D), kv_mod[:, D:].reshape(B, 1, D), kv_mod[:, :D].reshape(B, 1, D),
                       kv_w.astype(BF16), d_cos, d_sin, rope_tiles=D // TN_PROJ, scale=1.0, batch=B, seq=S)
        if l < N_A:
            z = _retention_layer(xf, gvec(l, 0), sc_a, sh_a, ret_cos, ret_sin, ret_w_in, l, batch=B, seq=S)
            wo, lo = ret_w_out, l
        else:
            lo = l - N_A
            q = _proj(xf, gvec(l, 0), sc_a, sh_a, diff_w_q[lo].astype(BF16), d_cos, d_sin,
                      rope_tiles=D // TN_PROJ, scale=DIFF_HEAD_DIM ** -0.5, batch=B, seq=S)
            z = _diff_attention(q, kv, diff_lam[lo], diff_subln_g[lo].reshape(1, DIFF_V_DIM),
                                lambda_init=0.8 - 0.6 * math.exp(-0.3 * l))
            wo = diff_w_o
        xf = _post_mlp(z, wo, lo, xf, ga_a, gvec(l, 1), gvec(l, 2), sc_m, sh_m, mlp_w1, mlp_w2, l, ga_m, gvec(l, 3),
                       seq=S)
    return xf.reshape(B, S, D)
```

```python
import functools
import math

import jax
import jax.numpy as jnp
from jax import lax
from jax.experimental import pallas as pl
from jax.experimental.pallas import tpu as pltpu

D_MODEL = 1024
DEPTH = 4
N_A = DEPTH // 2
RET_HEADS = 4
RET_QK_DIM = D_MODEL // RET_HEADS
RET_V_DIM = 2 * RET_QK_DIM
RET_ROPE_BASE = 10000.0
DIFF_HEAD_DIM = 64
DIFF_HEADS = D_MODEL // (2 * DIFF_HEAD_DIM)
DIFF_V_DIM = 2 * DIFF_HEAD_DIM
ROPE_THETA = 500000.0
ROPE_DIM = DIFF_HEAD_DIM // 4
D_FF = 4 * D_MODEL
EPS = 1e-6

LANES = 128
BF16_SUBLANES = 16
VMEM_LIMIT = 56 * 1024 * 1024

TM_PROJ = 1024
TN_PROJ = 1024
TM_MLP = 512
SUB_MLP = 256
TF_MLP = 512
RET_CHUNK = 256
TQ = 256
ATTN_LOOKAHEAD = 2
TM_TAB = 2048

F32 = jnp.float32
BF16 = jnp.bfloat16


def _cparams(*sem):
    return pltpu.CompilerParams(dimension_semantics=sem, vmem_limit_bytes=VMEM_LIMIT)


def _resident(stack, layer):
    return pl.BlockSpec((None,) + stack.shape[1:], lambda *_: (layer, 0, 0), pipeline_mode=pl.Buffered(1))


def _rms(x):
    return x * lax.rsqrt(jnp.mean(x * x, axis=-1, keepdims=True) + EPS)


def _dot(a, b):
    return jnp.dot(a, b, preferred_element_type=F32)


def _dot_nt(a, b):
    return lax.dot_general(a, b, (((1,), (1,)), ((), ())), preferred_element_type=F32)


def _dot_tn(a, b):
    return lax.dot_general(a, b, (((0,), (0,)), ((), ())), preferred_element_type=F32)


def _mod_kernel(c_ref, w_ref, b_ref, o_ref):
    c = c_ref[...]
    c_act = c * (1.0 / (1.0 + jnp.exp(-c)))
    o_ref[...] = _dot(c_act.astype(BF16), w_ref[...].astype(BF16)) + b_ref[...]


def _modulation(c, w, b, tn):
    L, D, N = w.shape
    B = c.shape[0]
    return pl.pallas_call(
        _mod_kernel,
        grid=(L, N // tn),
        in_specs=[
            pl.BlockSpec((B, D), lambda l, j: (0, 0)),
            pl.BlockSpec((None, D, tn), lambda l, j: (l, 0, j)),
            pl.BlockSpec((None, 1, tn), lambda l, j: (l, 0, j)),
        ],
        out_specs=pl.BlockSpec((None, B, tn), lambda l, j: (l, 0, j)),
        out_shape=jax.ShapeDtypeStruct((L, B, N), F32),
        compiler_params=_cparams("parallel", "parallel"),
        name="adaln_mod",
    )(c, w, b.reshape(L, 1, N))


def _tables_kernel(pos_ref, invr_ref, invd_ref, rc_ref, rs_ref, dc_ref, ds_ref):
    p = pos_ref[...].astype(F32)
    ang = p * invr_ref[...]
    rc_ref[...] = jnp.cos(ang)
    rs_ref[...] = jnp.sin(ang)
    angd = p * invd_ref[...]
    rotary = jnp.bitwise_and(lax.broadcasted_iota(jnp.int32, angd.shape, 1), DIFF_HEAD_DIM - 1) < ROPE_DIM
    dc_ref[...] = jnp.where(rotary, jnp.cos(angd), 1.0)
    ds_ref[...] = jnp.where(rotary, jnp.sin(angd), 0.0)


def _rope_tables(positions):
    T = positions.size
    inv_r = RET_ROPE_BASE ** (-jnp.arange(0, RET_QK_DIM, 2, dtype=F32) / RET_QK_DIM)
    inv_d = ROPE_THETA ** (-jnp.arange(0, ROPE_DIM, 2, dtype=F32) / ROPE_DIM)
    pat = jnp.concatenate([inv_d, inv_d, jnp.zeros((DIFF_HEAD_DIM - ROPE_DIM,), F32)])
    inv_d_lanes = jnp.tile(pat, LANES // DIFF_HEAD_DIM)
    tab = jax.ShapeDtypeStruct((T, LANES), F32)
    tspec = pl.BlockSpec((TM_TAB, LANES), lambda i: (i, 0))
    vspec = pl.BlockSpec((1, LANES), lambda i: (0, 0))
    return pl.pallas_call(
        _tables_kernel,
        grid=(T // TM_TAB,),
        in_specs=[pl.BlockSpec((TM_TAB, 1), lambda i: (i, 0)), vspec, vspec],
        out_specs=[tspec] * 4,
        out_shape=[tab] * 4,
        compiler_params=_cparams("parallel"),
        name="rope_tables",
    )(positions.reshape(T, 1), inv_r.reshape(1, LANES), inv_d_lanes.reshape(1, LANES))


def _rotate_half_matrix():
    n = 2 * LANES
    half = ROPE_DIM // 2
    row = lax.broadcasted_iota(jnp.int32, (n, n), 0)
    col = lax.broadcasted_iota(jnp.int32, (n, n), 1)
    cm = jnp.bitwise_and(col, DIFF_HEAD_DIM - 1)
    return jnp.where((cm < half) & (row == col + half), -1.0,
                     jnp.where((cm >= half) & (cm < ROPE_DIM) & (row == col - half), 1.0, 0.0)).astype(BF16)


def _proj_kernel(x_ref, g_ref, sc_ref, sh_ref, w_ref, c_ref, s_ref, rot_ref, o_ref, h_ref, *, rope_tiles, scale):
    j = pl.program_id(1)

    @pl.when(j == 0)
    def _():
        y = _rms(x_ref[...]) * g_ref[...]
        h_ref[...] = (y * (1.0 + sc_ref[...]) + sh_ref[...]).astype(BF16)

    acc = _dot(h_ref[...], w_ref[...])
    tn = acc.shape[1]

    @pl.when(j < rope_tiles)
    def _():
        c = c_ref[...]
        s = s_ref[...]
        for cc in range(tn // (2 * LANES)):
            xc = acc[:, cc * 2 * LANES:(cc + 1) * 2 * LANES]
            partner = _dot(xc.astype(BF16), rot_ref[...])
            for u in range(2):
                lanes = slice(u * LANES, (u + 1) * LANES)
                o_ref[2 * cc + u] = ((xc[:, lanes] * c + partner[:, lanes] * s) * scale).astype(o_ref.dtype)

    @pl.when(j >= rope_tiles)
    def _():
        for cc in range(tn // LANES):
            o_ref[cc] = acc[:, cc * LANES:(cc + 1) * LANES].astype(o_ref.dtype)


def _proj(x, g, sc, sh, w, c_tab, s_tab, *, rope_tiles, scale, batch, seq):
    T, D = x.shape
    N = w.shape[1]
    tpb = seq // TM_PROJ
    vec = pl.BlockSpec((1, D), lambda i, j: (0, 0))
    bvec = pl.BlockSpec((None, 1, D), lambda i, j: (i // tpb, 0, 0))
    tspec = pl.BlockSpec((TM_PROJ, LANES), lambda i, j: (i, 0))
    kern = functools.partial(_proj_kernel, rope_tiles=rope_tiles, scale=scale)
    return pl.pallas_call(
        kern,
        grid=(T // TM_PROJ, N // TN_PROJ),
        in_specs=[
            pl.BlockSpec((TM_PROJ, D), lambda i, j: (i, 0)), vec, bvec, bvec,
            pl.BlockSpec((D, TN_PROJ), lambda i, j: (0, j)), tspec, tspec,
            pl.BlockSpec((2 * LANES, 2 * LANES), lambda i, j: (0, 0)),
        ],
        out_specs=pl.BlockSpec((None, TN_PROJ // LANES, TM_PROJ, LANES), lambda i, j: (i // tpb, j, i % tpb, 0)),
        out_shape=jax.ShapeDtypeStruct((batch, N // LANES, seq, LANES), BF16),
        scratch_shapes=[pltpu.VMEM((TM_PROJ, D), BF16)],
        compiler_params=_cparams("parallel", "arbitrary"),
        name="proj_heads",
    )(x, g, sc, sh, w, c_tab, s_tab, _rotate_half_matrix())


def _ret_kernel(x_ref, g_ref, sc_ref, sh_ref, cos_ref, sin_ref, w_ref, z_ref, r_ref, dm_ref, xi_ref, ze_ref):
    C = RET_CHUNK
    HK = RET_HEADS * RET_QK_DIM
    HV = RET_HEADS * RET_V_DIM
    n = pl.program_id(1)
    log_g = [math.log1p(-(2.0 ** (-5 - h))) for h in range(RET_HEADS)]

    @pl.when(n == 0)
    def _():
        r_ref[...] = jnp.zeros_like(r_ref)
        diff = (lax.broadcasted_iota(jnp.int32, (C, C), 0) - lax.broadcasted_iota(jnp.int32, (C, C), 1)).astype(F32)
        idx = lax.broadcasted_iota(jnp.int32, (C, RET_V_DIM), 0).astype(F32)
        for h in range(RET_HEADS):
            dm_ref[h] = jnp.where(diff >= 0, jnp.exp(jnp.maximum(diff, 0.0) * log_g[h]), 0.0)
            xi_ref[h] = jnp.exp((idx + 1.0) * log_g[h])
            ze_ref[h] = jnp.exp((C - 1.0 - idx) * log_g[h])

    hin = ((_rms(x_ref[...]) * g_ref[...]) * (1.0 + sc_ref[...]) + sh_ref[...]).astype(BF16)
    cos = cos_ref[...]
    sin = sin_ref[...]

    def roped(col, mul):
        t = _dot(hin, w_ref[:, col:col + RET_QK_DIM])
        x1, x2 = t[:, :LANES], t[:, LANES:]
        return jnp.concatenate([(x1 * cos - x2 * sin) * mul, (x2 * cos + x1 * sin) * mul], axis=1).astype(BF16)

    def proj_qk(h):
        return roped(h * RET_QK_DIM, 1.0), roped(HK + h * RET_QK_DIM, RET_QK_DIM ** -0.5)

    def proj_vg(h):
        return (_dot(hin, w_ref[:, 2 * HK + h * RET_V_DIM:2 * HK + (h + 1) * RET_V_DIM]),
                _dot(hin, w_ref[:, 2 * HK + HV + h * RET_V_DIM:2 * HK + HV + (h + 1) * RET_V_DIM]))

    qk_next, vg_next = proj_qk(0), proj_vg(0)
    for h in range(RET_HEADS):
        (qh, kh), (vh, gt) = qk_next, vg_next
        if h + 1 < RET_HEADS:
            qk_next = proj_qk(h + 1)
        s = _dot_nt(qh, kh)
        rh = r_ref[h]
        cross = _dot(qh, rh.astype(BF16)) * xi_ref[h]
        r_ref[h] = rh * math.exp(C * log_g[h]) + _dot_tn(kh, (vh * ze_ref[h]).astype(BF16))
        if h + 1 < RET_HEADS:
            vg_next = proj_vg(h + 1)
        intra = _dot((s * dm_ref[h]).astype(BF16), vh.astype(BF16))
        o = _rms(intra + cross)
        z_ref[:, h * RET_V_DIM:(h + 1) * RET_V_DIM] = (gt * (1.0 / (1.0 + jnp.exp(-gt))) * o).astype(BF16)


def _retention_layer(x, g, sc, sh, cos, sin, w_in, l, *, batch, seq):
    T, D = x.shape
    C = RET_CHUNK
    nch = seq // C
    HV = RET_HEADS * RET_V_DIM
    N = w_in.shape[2]
    row = lambda b, n: (b * nch + n, 0)
    vec = pl.BlockSpec((1, D), lambda b, n: (0, 0))
    bvec = pl.BlockSpec((None, 1, D), lambda b, n: (b, 0, 0))
    tspec = pl.BlockSpec((C, LANES), row)
    return pl.pallas_call(
        _ret_kernel,
        grid=(batch, nch),
        in_specs=[
            pl.BlockSpec((C, D), row), vec, bvec, bvec, tspec, tspec,
            pl.BlockSpec((None, D, N), lambda b, n: (l, 0, 0), pipeline_mode=pl.Buffered(1)),
        ],
        out_specs=pl.BlockSpec((C, HV), row),
        out_shape=jax.ShapeDtypeStruct((T, HV), BF16),
        scratch_shapes=[
            pltpu.VMEM((RET_HEADS, RET_QK_DIM, RET_V_DIM), F32),
            pltpu.VMEM((RET_HEADS, C, C), F32),
            pltpu.VMEM((RET_HEADS, C, RET_V_DIM), F32),
            pltpu.VMEM((RET_HEADS, C, RET_V_DIM), F32),
        ],
        compiler_params=_cparams("arbitrary", "arbitrary"),
        name="retention",
    )(x, g, sc, sh, cos, sin, w_in)


def _attn_kernel(lam_ref, q_ref, k_ref, v_ref, sg_ref, z_ref, vt_ref, *, lambda_init):
    S = q_ref.shape[0]
    lf = lam_ref[...]
    lam = (jnp.exp(jnp.sum(lf[0:1] * lf[1:2], axis=-1, keepdims=True))
           - jnp.exp(jnp.sum(lf[2:3] * lf[3:4], axis=-1, keepdims=True)) + lambda_init)
    lane = lax.broadcasted_iota(jnp.int32, (TQ, LANES), 1)
    causal_t = lax.broadcasted_iota(jnp.int32, (TQ, TQ), 0) <= lax.broadcasted_iota(jnp.int32, (TQ, TQ), 1)
    vt_ref[:DIFF_V_DIM, :] = v_ref[...].T
    vt_ref[DIFF_V_DIM:, :] = jnp.ones((vt_ref.shape[0] - DIFF_V_DIM, S), BF16)

    def scores_t(i):
        n_keys = (i + 1) * TQ
        qp = q_ref[i * TQ:(i + 1) * TQ, :]
        zero = jnp.zeros_like(qp)
        keys = k_ref[:n_keys, :]
        return (_dot_nt(keys, jnp.where(lane < DIFF_HEAD_DIM, qp, zero)),
                _dot_nt(keys, jnp.where(lane < DIFF_HEAD_DIM, zero, qp)))

    def softmax_numerators_t(st):
        n_keys = st.shape[0]
        sd = jnp.where(causal_t, st[n_keys - TQ:, :], -jnp.inf)
        st = sd if n_keys == TQ else jnp.concatenate([st[:n_keys - TQ, :], sd], axis=0)
        return jnp.exp2(st - jnp.max(st, axis=0, keepdims=True)).astype(BF16)

    order = list(range(S // TQ))
    ahead = [scores_t(i) for i in order[:ATTN_LOOKAHEAD]]
    for pos, i in enumerate(order):
        n_keys = (i + 1) * TQ
        st1, st2 = ahead.pop(0)
        if pos + ATTN_LOOKAHEAD < len(order):
            ahead.append(scores_t(order[pos + ATTN_LOOKAHEAD]))
        r1 = _dot(vt_ref[:, :n_keys], softmax_numerators_t(st1))
        r2 = _dot(vt_ref[:, :n_keys], softmax_numerators_t(st2))
        ot = (r1[:DIFF_V_DIM] * (1.0 / r1[DIFF_V_DIM:DIFF_V_DIM + 1])
              - lam * (r2[:DIFF_V_DIM] * (1.0 / r2[DIFF_V_DIM:DIFF_V_DIM + 1])))
        z_ref[i * TQ:(i + 1) * TQ, :] = ((_rms(ot.T) * sg_ref[...]) * (1.0 - lambda_init)).astype(BF16)


def _diff_attention(q, kv, lam, sg, *, lambda_init):
    B, H, S, _ = q.shape
    slab = lambda off: pl.BlockSpec((None, None, S, LANES), lambda b, p: (b, p + off, 0, 0))
    kern = functools.partial(_attn_kernel, lambda_init=lambda_init)
    return pl.pallas_call(
        kern,
        grid=(B, H),
        in_specs=[
            pl.BlockSpec(lam.shape, lambda b, p: (0, 0)),
            slab(0), slab(0), slab(H),
            pl.BlockSpec((1, DIFF_V_DIM), lambda b, p: (0, 0)),
        ],
        out_specs=pl.BlockSpec((S, LANES), lambda b, p: (b, p)),
        out_shape=jax.ShapeDtypeStruct((B * S, H * LANES), BF16),
        scratch_shapes=[pltpu.VMEM((DIFF_V_DIM + BF16_SUBLANES, S), BF16)],
        compiler_params=_cparams("parallel", "parallel"),
        name="diff_attention",
    )(lam, q, kv, kv, sg)


def _post_mlp_kernel(z_ref, wo_ref, x_ref, gaa_ref, g1_ref, g2_ref, scm_ref, shm_ref, w1_ref, w2_ref, gam_ref, g3_ref,
                     xo_ref, a_ref):
    halves = [slice(r * SUB_MLP, (r + 1) * SUB_MLP) for r in range(TM_MLP // SUB_MLP)]
    ys = [_dot(z_ref[rows, :], wo_ref[...]) for rows in halves]
    hs = []
    for rows, y in zip(halves, ys):
        xn = x_ref[rows, :] + (1.0 + gaa_ref[...]) * (_rms(y) * g1_ref[...])
        xo_ref[rows, :] = xn
        hs.append(((_rms(xn) * g2_ref[...]) * (1.0 + scm_ref[...]) + shm_ref[...]).astype(BF16))
    for rows, h in zip(halves, hs):
        for f in range(a_ref.shape[1] // TF_MLP):
            cols = slice(f * TF_MLP, (f + 1) * TF_MLP)
            a = jnp.maximum(_dot(h, w1_ref[:, cols]), 0.0)
            a_ref[rows, cols] = (a * a).astype(BF16)
    y2s = [_dot(a_ref[rows, :], w2_ref[...]) for rows in halves]
    for rows, y2 in zip(halves, y2s):
        xo_ref[rows, :] = xo_ref[rows, :] + (1.0 + gam_ref[...]) * (_rms(y2) * g3_ref[...])


def _post_mlp(z, wo, lo, x, gaa, g1, g2, scm, shm, w1, w2, l, gam, g3, *, seq):
    T, D = x.shape
    KZ = z.shape[1]
    F = w1.shape[2]
    tpb = seq // TM_MLP
    xspec = pl.BlockSpec((TM_MLP, D), lambda i: (i, 0))
    vec = pl.BlockSpec((1, D), lambda i: (0, 0))
    bvec = pl.BlockSpec((None, 1, D), lambda i: (i // tpb, 0, 0))
    return pl.pallas_call(
        _post_mlp_kernel,
        grid=(T // TM_MLP,),
        in_specs=[
            pl.BlockSpec((TM_MLP, KZ), lambda i: (i, 0)), _resident(wo, lo), xspec,
            bvec, vec, vec, bvec, bvec,
            _resident(w1, l), _resident(w2, l), bvec, vec,
        ],
        out_specs=xspec,
        out_shape=jax.ShapeDtypeStruct((T, D), F32),
        scratch_shapes=[pltpu.VMEM((TM_MLP, F), BF16)],
        compiler_params=_cparams("parallel"),
        name="post_mlp",
    )(z, wo, x, gaa, g1, g2, scm, shm, w1, w2, gam, g3)


def kernel(x, c, positions, norm_g, ada_w, ada_b, ret_w_in, ret_w_out, kv_norm_g, kv_ada_w, kv_ada_b, kv_w,
           diff_w_q, diff_w_o, diff_lam, diff_subln_g, mlp_w1, mlp_w2):
    B, S, D = x.shape
    T = B * S
    xf = x.reshape(T, D)

    mod = _modulation(c, ada_w, ada_b, 1536)
    kv_mod = _modulation(c, kv_ada_w[None], kv_ada_b[None], 1024)[0]
    ret_cos, ret_sin, d_cos, d_sin = _rope_tables(positions)

    def mvec(l, i):
        return mod[l, :, i * D:(i + 1) * D].reshape(B, 1, D)

    gvec = lambda l, i: norm_g[l, i].reshape(1, D)
    kv = None
    ret_w_in, ret_w_out, diff_w_o, mlp_w1, mlp_w2 = (
        w.astype(BF16) for w in (ret_w_in, ret_w_out, diff_w_o, mlp_w1, mlp_w2))

    for l in range(DEPTH):
        sh_a, sc_a, ga_a, sh_m, sc_m, ga_m = (mvec(l, i) for i in range(6))
        if l == N_A:
            kv = _proj(xf, kv_norm_g.reshape(1, D), kv_mod[:, D:].reshape(B, 1, D), kv_mod[:, :D].reshape(B, 1, D),
                       kv_w.astype(BF16), d_cos, d_sin, rope_tiles=D // TN_PROJ, scale=1.0, batch=B, seq=S)
        if l < N_A:
            z = _retention_layer(xf, gvec(l, 0), sc_a, sh_a, ret_cos, ret_sin, ret_w_in, l, batch=B, seq=S)
            wo, lo = ret_w_out, l
        else:
            lo = l - N_A
            q = _proj(xf, gvec(l, 0), sc_a, sh_a, diff_w_q[lo].astype(BF16), d_cos, d_sin,
                      rope_tiles=D // TN_PROJ, scale=DIFF_HEAD_DIM ** -0.5 * math.log2(math.e), batch=B, seq=S)
            z = _diff_attention(q, kv, diff_lam[lo], diff_subln_g[lo].reshape(1, DIFF_V_DIM),
                                lambda_init=0.8 - 0.6 * math.exp(-0.3 * l))
            wo = diff_w_o
        xf = _post_mlp(z, wo, lo, xf, ga_a, gvec(l, 1), gvec(l, 2), sc_m, sh_m, mlp_w1, mlp_w2, l, ga_m, gvec(l, 3),
                       seq=S)
    return xf.reshape(B, S, D)
```

```python
import functools
import math

import jax
import jax.numpy as jnp
from jax import lax
from jax.experimental import pallas as pl
from jax.experimental.pallas import tpu as pltpu

D_MODEL = 1024
DEPTH = 4
N_A = DEPTH // 2
RET_HEADS = 4
RET_QK_DIM = D_MODEL // RET_HEADS
RET_V_DIM = 2 * RET_QK_DIM
RET_ROPE_BASE = 10000.0
DIFF_HEAD_DIM = 64
DIFF_HEADS = D_MODEL // (2 * DIFF_HEAD_DIM)
DIFF_V_DIM = 2 * DIFF_HEAD_DIM
ROPE_THETA = 500000.0
ROPE_DIM = DIFF_HEAD_DIM // 4
D_FF = 4 * D_MODEL
EPS = 1e-6

LANES = 128
BF16_SUBLANES = 16
VMEM_LIMIT = 56 * 1024 * 1024

TM_PROJ = 1024
SUB_PROJ = 512
TN_PROJ = 512
TM_MLP = 512
SUB_MLP = 256
TF_MLP = 512
RET_CHUNK = 256
RET_CHUNKS_PER_STEP = 2
TQ = 256
ATTN_PAIRS = 2
ATTN_LOOKAHEAD = 4
TM_TAB = 2048

F32 = jnp.float32
BF16 = jnp.bfloat16


def _cparams(*sem):
    return pltpu.CompilerParams(dimension_semantics=sem, vmem_limit_bytes=VMEM_LIMIT)


def _resident(stack, layer):
    return pl.BlockSpec((None,) + stack.shape[1:], lambda *_: (layer, 0, 0), pipeline_mode=pl.Buffered(1))


def _rms(x):
    return x * lax.rsqrt(jnp.mean(x * x, axis=-1, keepdims=True) + EPS)


def _dot(a, b):
    return jnp.dot(a, b, preferred_element_type=F32)


def _dot_nt(a, b):
    return lax.dot_general(a, b, (((1,), (1,)), ((), ())), preferred_element_type=F32)


def _dot_tn(a, b):
    return lax.dot_general(a, b, (((0,), (0,)), ((), ())), preferred_element_type=F32)


def _mod_kernel(c_ref, w_ref, b_ref, o_ref):
    c = c_ref[...]
    c_act = c * (1.0 / (1.0 + jnp.exp(-c)))
    o_ref[...] = _dot(c_act.astype(BF16), w_ref[...].astype(BF16)) + b_ref[...]


def _modulation(c, w, b, tn):
    L, D, N = w.shape
    B = c.shape[0]
    return pl.pallas_call(
        _mod_kernel,
        grid=(L, N // tn),
        in_specs=[
            pl.BlockSpec((B, D), lambda l, j: (0, 0)),
            pl.BlockSpec((None, D, tn), lambda l, j: (l, 0, j)),
            pl.BlockSpec((None, 1, tn), lambda l, j: (l, 0, j)),
        ],
        out_specs=pl.BlockSpec((None, B, tn), lambda l, j: (l, 0, j)),
        out_shape=jax.ShapeDtypeStruct((L, B, N), F32),
        compiler_params=_cparams("parallel", "parallel"),
        name="adaln_mod",
    )(c, w, b.reshape(L, 1, N))


def _tables_kernel(pos_ref, invr_ref, invd_ref, rc_ref, rs_ref, dc_ref, ds_ref):
    p = pos_ref[...].astype(F32)
    ang = p * invr_ref[...]
    rc_ref[...] = jnp.cos(ang)
    rs_ref[...] = jnp.sin(ang)
    angd = p * invd_ref[...]
    rotary = jnp.bitwise_and(lax.broadcasted_iota(jnp.int32, angd.shape, 1), DIFF_HEAD_DIM - 1) < ROPE_DIM
    dc_ref[...] = jnp.where(rotary, jnp.cos(angd), 1.0)
    ds_ref[...] = jnp.where(rotary, jnp.sin(angd), 0.0)


def _rope_tables(positions):
    T = positions.size
    inv_r = RET_ROPE_BASE ** (-jnp.arange(0, RET_QK_DIM, 2, dtype=F32) / RET_QK_DIM)
    inv_d = ROPE_THETA ** (-jnp.arange(0, ROPE_DIM, 2, dtype=F32) / ROPE_DIM)
    pat = jnp.concatenate([inv_d, inv_d, jnp.zeros((DIFF_HEAD_DIM - ROPE_DIM,), F32)])
    inv_d_lanes = jnp.tile(pat, LANES // DIFF_HEAD_DIM)
    tab = jax.ShapeDtypeStruct((T, LANES), F32)
    tspec = pl.BlockSpec((TM_TAB, LANES), lambda i: (i, 0))
    vspec = pl.BlockSpec((1, LANES), lambda i: (0, 0))
    return pl.pallas_call(
        _tables_kernel,
        grid=(T // TM_TAB,),
        in_specs=[pl.BlockSpec((TM_TAB, 1), lambda i: (i, 0)), vspec, vspec],
        out_specs=[tspec] * 4,
        out_shape=[tab] * 4,
        compiler_params=_cparams("parallel"),
        name="rope_tables",
    )(positions.reshape(T, 1), inv_r.reshape(1, LANES), inv_d_lanes.reshape(1, LANES))


def _rotate_half_matrix():
    n = 2 * LANES
    half = ROPE_DIM // 2
    row = lax.broadcasted_iota(jnp.int32, (n, n), 0)
    col = lax.broadcasted_iota(jnp.int32, (n, n), 1)
    cm = jnp.bitwise_and(col, DIFF_HEAD_DIM - 1)
    return jnp.where((cm < half) & (row == col + half), -1.0,
                     jnp.where((cm >= half) & (cm < ROPE_DIM) & (row == col - half), 1.0, 0.0)).astype(BF16)


def _proj_kernel(x_ref, g_ref, sc_ref, sh_ref, w_ref, c_ref, s_ref, rot_ref, o_ref, *, rope_cols, scale):
    pair = 2 * LANES
    halves = [slice(r * SUB_PROJ, (r + 1) * SUB_PROJ) for r in range(TM_PROJ // SUB_PROJ)]
    hs = [((_rms(x_ref[rows, :]) * g_ref[...]) * (1.0 + sc_ref[...]) + sh_ref[...]).astype(BF16) for rows in halves]
    units = [(r, cc) for r in range(len(halves)) for cc in range(w_ref.shape[1] // TN_PROJ)]

    def main_dot(u):
        r, cc = units[u]
        return _dot(hs[r], w_ref[:, cc * TN_PROJ:(cc + 1) * TN_PROJ])

    acc_next = main_dot(0)
    for u, (r, cc) in enumerate(units):
        acc = acc_next
        if u + 1 < len(units):
            acc_next = main_dot(u + 1)
        rows = halves[r]
        for p in range(TN_PROJ // pair):
            xc = acc[:, p * pair:(p + 1) * pair]
            roped = cc * TN_PROJ + p * pair < rope_cols
            if roped:
                partner = _dot(xc.astype(BF16), rot_ref[...])
            for v in range(2):
                lanes = slice(v * LANES, (v + 1) * LANES)
                slab = (cc * TN_PROJ + p * pair) // LANES + v
                if roped:
                    o_ref[slab, rows, :] = ((xc[:, lanes] * c_ref[rows, :] + partner[:, lanes] * s_ref[rows, :])
                                            * scale).astype(BF16)
                else:
                    o_ref[slab, rows, :] = xc[:, lanes].astype(BF16)


def _proj(x, g, sc, sh, w, c_tab, s_tab, *, rope_cols, scale, batch, seq):
    T, D = x.shape
    N = w.shape[1]
    tpb = seq // TM_PROJ
    vec = pl.BlockSpec((1, D), lambda i: (0, 0))
    bvec = pl.BlockSpec((None, 1, D), lambda i: (i // tpb, 0, 0))
    tspec = pl.BlockSpec((TM_PROJ, LANES), lambda i: (i, 0))
    kern = functools.partial(_proj_kernel, rope_cols=rope_cols, scale=scale)
    return pl.pallas_call(
        kern,
        grid=(T // TM_PROJ,),
        in_specs=[
            pl.BlockSpec((TM_PROJ, D), lambda i: (i, 0)), vec, bvec, bvec,
            pl.BlockSpec((D, N), lambda i: (0, 0), pipeline_mode=pl.Buffered(1)), tspec, tspec,
            pl.BlockSpec((2 * LANES, 2 * LANES), lambda i: (0, 0)),
        ],
        out_specs=pl.BlockSpec((None, N // LANES, TM_PROJ, LANES), lambda i: (i // tpb, 0, i % tpb, 0)),
        out_shape=jax.ShapeDtypeStruct((batch, N // LANES, seq, LANES), BF16),
        compiler_params=_cparams("parallel"),
        name="proj_heads",
    )(x, g, sc, sh, w, c_tab, s_tab, _rotate_half_matrix())


def _ret_kernel(x_ref, g_ref, sc_ref, sh_ref, cos_ref, sin_ref, w_ref, z_ref, r_ref, dm_ref, xi_ref, ze_ref):
    C = RET_CHUNK
    n = pl.program_id(1)
    log_g = [math.log1p(-(2.0 ** (-5 - h))) for h in range(RET_HEADS)]

    @pl.when(n == 0)
    def _():
        r_ref[...] = jnp.zeros_like(r_ref)
        diff = (lax.broadcasted_iota(jnp.int32, (C, C), 0) - lax.broadcasted_iota(jnp.int32, (C, C), 1)).astype(F32)
        idx = lax.broadcasted_iota(jnp.int32, (C, RET_V_DIM), 0).astype(F32)
        for h in range(RET_HEADS):
            dm_ref[h] = jnp.where(diff >= 0, jnp.exp(jnp.maximum(diff, 0.0) * log_g[h]), 0.0)
            xi_ref[h] = jnp.exp((idx + 1.0) * log_g[h])
            ze_ref[h] = jnp.exp((C - 1.0 - idx) * log_g[h])

    for ci in range(RET_CHUNKS_PER_STEP):
        _ret_chunk(slice(ci * C, (ci + 1) * C), log_g, x_ref, g_ref, sc_ref, sh_ref, cos_ref, sin_ref, w_ref, z_ref,
                   r_ref, dm_ref, xi_ref, ze_ref)


def _ret_chunk(rows, log_g, x_ref, g_ref, sc_ref, sh_ref, cos_ref, sin_ref, w_ref, z_ref, r_ref, dm_ref, xi_ref, ze_ref):
    C = RET_CHUNK
    HK = RET_HEADS * RET_QK_DIM
    HV = RET_HEADS * RET_V_DIM
    hin = ((_rms(x_ref[rows, :]) * g_ref[...]) * (1.0 + sc_ref[...]) + sh_ref[...]).astype(BF16)
    cos = cos_ref[rows, :]
    sin = sin_ref[rows, :]

    def roped(col, mul):
        t = _dot(hin, w_ref[:, col:col + RET_QK_DIM])
        x1, x2 = t[:, :LANES], t[:, LANES:]
        return jnp.concatenate([(x1 * cos - x2 * sin) * mul, (x2 * cos + x1 * sin) * mul], axis=1).astype(BF16)

    for h in range(RET_HEADS):
        qh = roped(h * RET_QK_DIM, 1.0)
        kh = roped(HK + h * RET_QK_DIM, RET_QK_DIM ** -0.5)
        vh = _dot(hin, w_ref[:, 2 * HK + h * RET_V_DIM:2 * HK + (h + 1) * RET_V_DIM])
        gt = _dot(hin, w_ref[:, 2 * HK + HV + h * RET_V_DIM:2 * HK + HV + (h + 1) * RET_V_DIM])
        s = _dot_nt(qh, kh)
        rh = r_ref[h]
        cross = _dot(qh, rh.astype(BF16)) * xi_ref[h]
        r_ref[h] = rh * math.exp(C * log_g[h]) + _dot_tn(kh, (vh * ze_ref[h]).astype(BF16))
        intra = _dot((s * dm_ref[h]).astype(BF16), vh.astype(BF16))
        o = _rms(intra + cross)
        z_ref[rows, h * RET_V_DIM:(h + 1) * RET_V_DIM] = (gt * (1.0 / (1.0 + jnp.exp(-gt))) * o).astype(BF16)


def _retention_layer(x, g, sc, sh, cos, sin, w_in, l, *, batch, seq):
    T, D = x.shape
    C = RET_CHUNK
    rows = RET_CHUNKS_PER_STEP * C
    nch = seq // rows
    HV = RET_HEADS * RET_V_DIM
    N = w_in.shape[2]
    row = lambda b, n: (b * nch + n, 0)
    vec = pl.BlockSpec((1, D), lambda b, n: (0, 0))
    bvec = pl.BlockSpec((None, 1, D), lambda b, n: (b, 0, 0))
    tspec = pl.BlockSpec((rows, LANES), row)
    return pl.pallas_call(
        _ret_kernel,
        grid=(batch, nch),
        in_specs=[
            pl.BlockSpec((rows, D), row), vec, bvec, bvec, tspec, tspec,
            pl.BlockSpec((None, D, N), lambda b, n: (l, 0, 0), pipeline_mode=pl.Buffered(1)),
        ],
        out_specs=pl.BlockSpec((rows, HV), row),
        out_shape=jax.ShapeDtypeStruct((T, HV), BF16),
        scratch_shapes=[
            pltpu.VMEM((RET_HEADS, RET_QK_DIM, RET_V_DIM), F32),
            pltpu.VMEM((RET_HEADS, C, C), F32),
            pltpu.VMEM((RET_HEADS, C, RET_V_DIM), F32),
            pltpu.VMEM((RET_HEADS, C, RET_V_DIM), F32),
        ],
        compiler_params=_cparams("arbitrary", "arbitrary"),
        name="retention",
    )(x, g, sc, sh, cos, sin, w_in)


def _attn_kernel(lam_ref, q_ref, k_ref, v_ref, sg_ref, z_ref, vt_ref, *, lambda_init):
    S = q_ref.shape[1]
    lf = lam_ref[...]
    lam = (jnp.exp(jnp.sum(lf[0:1] * lf[1:2], axis=-1, keepdims=True))
           - jnp.exp(jnp.sum(lf[2:3] * lf[3:4], axis=-1, keepdims=True)) + lambda_init)
    lane = lax.broadcasted_iota(jnp.int32, (TQ, LANES), 1)
    causal_t = lax.broadcasted_iota(jnp.int32, (TQ, TQ), 0) <= lax.broadcasted_iota(jnp.int32, (TQ, TQ), 1)
    for pr in range(ATTN_PAIRS):
        vt_ref[pr, :DIFF_V_DIM, :] = v_ref[pr].T
        vt_ref[pr, DIFF_V_DIM:, :] = jnp.ones((vt_ref.shape[1] - DIFF_V_DIM, S), BF16)

    def scores_t(unit):
        i, pr = unit
        n_keys = (i + 1) * TQ
        qp = q_ref[pr, i * TQ:(i + 1) * TQ, :]
        zero = jnp.zeros_like(qp)
        keys = k_ref[pr, :n_keys, :]
        return (_dot_nt(keys, jnp.where(lane < DIFF_HEAD_DIM, qp, zero)),
                _dot_nt(keys, jnp.where(lane < DIFF_HEAD_DIM, zero, qp)))

    def softmax_numerators_t(st):
        n_keys = st.shape[0]
        sd = jnp.where(causal_t, st[n_keys - TQ:, :], -jnp.inf)
        st = sd if n_keys == TQ else jnp.concatenate([st[:n_keys - TQ, :], sd], axis=0)
        return jnp.exp2(st - jnp.max(st, axis=0, keepdims=True)).astype(BF16)

    units = [(i, pr) for i in range(S // TQ) for pr in range(ATTN_PAIRS)]
    ahead = [scores_t(u) for u in units[:ATTN_LOOKAHEAD]]
    for pos, (i, pr) in enumerate(units):
        n_keys = (i + 1) * TQ
        st1, st2 = ahead.pop(0)
        if pos + ATTN_LOOKAHEAD < len(units):
            ahead.append(scores_t(units[pos + ATTN_LOOKAHEAD]))
        r1 = _dot(vt_ref[pr, :, :n_keys], softmax_numerators_t(st1))
        r2 = _dot(vt_ref[pr, :, :n_keys], softmax_numerators_t(st2))
        ot = (r1[:DIFF_V_DIM] * (1.0 / r1[DIFF_V_DIM:DIFF_V_DIM + 1])
              - lam * (r2[:DIFF_V_DIM] * (1.0 / r2[DIFF_V_DIM:DIFF_V_DIM + 1])))
        z_ref[i * TQ:(i + 1) * TQ, pr * LANES:(pr + 1) * LANES] = (
            (_rms(ot.T) * sg_ref[...]) * (1.0 - lambda_init)).astype(BF16)


def _diff_attention(q, kv, lam, sg, *, lambda_init):
    B, H, S, _ = q.shape
    hp = H // ATTN_PAIRS
    slab = lambda off: pl.BlockSpec((None, ATTN_PAIRS, S, LANES), lambda b, p: (b, p + off, 0, 0))
    kern = functools.partial(_attn_kernel, lambda_init=lambda_init)
    return pl.pallas_call(
        kern,
        grid=(B, hp),
        in_specs=[
            pl.BlockSpec(lam.shape, lambda b, p: (0, 0)),
            slab(0), slab(0), slab(hp),
            pl.BlockSpec((1, DIFF_V_DIM), lambda b, p: (0, 0)),
        ],
        out_specs=pl.BlockSpec((S, ATTN_PAIRS * LANES), lambda b, p: (b, p)),
        out_shape=jax.ShapeDtypeStruct((B * S, H * LANES), BF16),
        scratch_shapes=[pltpu.VMEM((ATTN_PAIRS, DIFF_V_DIM + BF16_SUBLANES, S), BF16)],
        compiler_params=_cparams("parallel", "parallel"),
        name="diff_attention",
    )(lam, q, kv, kv, sg)


def _post_mlp_kernel(z_ref, wo_ref, x_ref, gaa_ref, g1_ref, g2_ref, scm_ref, shm_ref, w1_ref, w2_ref, gam_ref, g3_ref,
                     xo_ref, a_ref):
    halves = [slice(r * SUB_MLP, (r + 1) * SUB_MLP) for r in range(TM_MLP // SUB_MLP)]
    ys = [_dot(z_ref[rows, :], wo_ref[...]) for rows in halves]
    hs = []
    for rows, y in zip(halves, ys):
        xn = x_ref[rows, :] + (1.0 + gaa_ref[...]) * (_rms(y) * g1_ref[...])
        xo_ref[rows, :] = xn
        hs.append(((_rms(xn) * g2_ref[...]) * (1.0 + scm_ref[...]) + shm_ref[...]).astype(BF16))
    for rows, h in zip(halves, hs):
        for f in range(a_ref.shape[1] // TF_MLP):
            cols = slice(f * TF_MLP, (f + 1) * TF_MLP)
            a = jnp.maximum(_dot(h, w1_ref[:, cols]), 0.0)
            a_ref[rows, cols] = (a * a).astype(BF16)
    y2s = [_dot(a_ref[rows, :], w2_ref[...]) for rows in halves]
    for rows, y2 in zip(halves, y2s):
        xo_ref[rows, :] = xo_ref[rows, :] + (1.0 + gam_ref[...]) * (_rms(y2) * g3_ref[...])


def _post_mlp(z, wo, lo, x, gaa, g1, g2, scm, shm, w1, w2, l, gam, g3, *, seq):
    T, D = x.shape
    KZ = z.shape[1]
    F = w1.shape[2]
    tpb = seq // TM_MLP
    xspec = pl.BlockSpec((TM_MLP, D), lambda i: (i, 0))
    vec = pl.BlockSpec((1, D), lambda i: (0, 0))
    bvec = pl.BlockSpec((None, 1, D), lambda i: (i // tpb, 0, 0))
    return pl.pallas_call(
        _post_mlp_kernel,
        grid=(T // TM_MLP,),
        in_specs=[
            pl.BlockSpec((TM_MLP, KZ), lambda i: (i, 0)), _resident(wo, lo), xspec,
            bvec, vec, vec, bvec, bvec,
            _resident(w1, l), _resident(w2, l), bvec, vec,
        ],
        out_specs=xspec,
        out_shape=jax.ShapeDtypeStruct((T, D), F32),
        scratch_shapes=[pltpu.VMEM((TM_MLP, F), BF16)],
        compiler_params=_cparams("parallel"),
        name="post_mlp",
    )(z, wo, x, gaa, g1, g2, scm, shm, w1, w2, gam, g3)


def kernel(x, c, positions, norm_g, ada_w, ada_b, ret_w_in, ret_w_out, kv_norm_g, kv_ada_w, kv_ada_b, kv_w,
           diff_w_q, diff_w_o, diff_lam, diff_subln_g, mlp_w1, mlp_w2):
    B, S, D = x.shape
    T = B * S
    xf = x.reshape(T, D)

    mod = _modulation(c, ada_w, ada_b, 1536)
    kv_mod = _modulation(c, kv_ada_w[None], kv_ada_b[None], 1024)[0]
    ret_cos, ret_sin, d_cos, d_sin = _rope_tables(positions)

    def mvec(l, i):
        return mod[l, :, i * D:(i + 1) * D].reshape(B, 1, D)

    gvec = lambda l, i: norm_g[l, i].reshape(1, D)
    kv = None
    ret_w_in, ret_w_out, diff_w_o, mlp_w1, mlp_w2 = (
        w.astype(BF16) for w in (ret_w_in, ret_w_out, diff_w_o, mlp_w1, mlp_w2))

    for l in range(DEPTH):
        sh_a, sc_a, ga_a, sh_m, sc_m, ga_m = (mvec(l, i) for i in range(6))
        if l == N_A:
            kv = _proj(xf, kv_norm_g.reshape(1, D), kv_mod[:, D:].reshape(B, 1, D), kv_mod[:, :D].reshape(B, 1, D),
                       kv_w.astype(BF16), d_cos, d_sin, rope_cols=D, scale=1.0, batch=B, seq=S)
        if l < N_A:
            z = _retention_layer(xf, gvec(l, 0), sc_a, sh_a, ret_cos, ret_sin, ret_w_in, l, batch=B, seq=S)
            wo, lo = ret_w_out, l
        else:
            lo = l - N_A
            q = _proj(xf, gvec(l, 0), sc_a, sh_a, diff_w_q[lo].astype(BF16), d_cos, d_sin,
                      rope_cols=D, scale=DIFF_HEAD_DIM ** -0.5 * math.log2(math.e), batch=B, seq=S)
            z = _diff_attention(q, kv, diff_lam[lo], diff_subln_g[lo].reshape(1, DIFF_V_DIM),
                                lambda_init=0.8 - 0.6 * math.exp(-0.3 * l))
            wo = diff_w_o
        xf = _post_mlp(z, wo, lo, xf, ga_a, gvec(l, 1), gvec(l, 2), sc_m, sh_m, mlp_w1, mlp_w2, l, ga_m, gvec(l, 3),
                       seq=S)
    return xf.reshape(B, S, D)
```

```python
import functools
import math

import jax
import jax.numpy as jnp
from jax import lax
from jax.experimental import pallas as pl
from jax.experimental.pallas import tpu as pltpu

D_MODEL = 1024
DEPTH = 4
N_A = DEPTH // 2
RET_HEADS = 4
RET_QK_DIM = D_MODEL // RET_HEADS
RET_V_DIM = 2 * RET_QK_DIM
RET_ROPE_BASE = 10000.0
DIFF_HEAD_DIM = 64
DIFF_HEADS = D_MODEL // (2 * DIFF_HEAD_DIM)
DIFF_V_DIM = 2 * DIFF_HEAD_DIM
ROPE_THETA = 500000.0
ROPE_DIM = DIFF_HEAD_DIM // 4
D_FF = 4 * D_MODEL
EPS = 1e-6

LANES = 128
BF16_SUBLANES = 16
VMEM_LIMIT = 56 * 1024 * 1024

TM_PROJ = 1024
SUB_PROJ = 512
TN_PROJ = 512
TM_MLP = 512
SUB_MLP = 256
TF_MLP = 512
RET_CHUNK = 256
RET_CHUNKS_PER_STEP = 2
TQ = 256
ATTN_PAIRS = 2
ATTN_LOOKAHEAD = 4
TM_TAB = 2048

F32 = jnp.float32
BF16 = jnp.bfloat16


def _cparams(*sem):
    return pltpu.CompilerParams(dimension_semantics=sem, vmem_limit_bytes=VMEM_LIMIT)


def _resident(w):
    return pl.BlockSpec(w.shape, lambda *_: (0, 0), pipeline_mode=pl.Buffered(1))


class _Casts:
    def __init__(self, sources, steps, step_of):
        self.arrays = [stack for stack, _ in sources]
        self.in_specs, self.out_specs, self.out_shapes = [], [], []
        for stack, layer in sources:
            _, rows, cols = stack.shape
            blk = rows // steps
            self.in_specs.append(pl.BlockSpec((None, blk, cols), lambda *g, layer=layer: (layer, step_of(*g), 0)))
            self.out_specs.append(pl.BlockSpec((blk, cols), lambda *g: (step_of(*g), 0)))
            self.out_shapes.append(jax.ShapeDtypeStruct((rows, cols), BF16))
        self.n = len(sources)

    @staticmethod
    def run(src_refs, dst_refs):
        for src, dst in zip(src_refs, dst_refs):
            dst[...] = src[...].astype(BF16)


def _rms(x):
    return x * lax.rsqrt(jnp.mean(x * x, axis=-1, keepdims=True) + EPS)


def _dot(a, b):
    return jnp.dot(a, b, preferred_element_type=F32)


def _dot_nt(a, b):
    return lax.dot_general(a, b, (((1,), (1,)), ((), ())), preferred_element_type=F32)


def _dot_tn(a, b):
    return lax.dot_general(a, b, (((0,), (0,)), ((), ())), preferred_element_type=F32)


def _mod_kernel(c_ref, w_ref, b_ref, o_ref):
    c = c_ref[...]
    c_act = c * (1.0 / (1.0 + jnp.exp(-c)))
    o_ref[...] = _dot(c_act.astype(BF16), w_ref[...].astype(BF16)) + b_ref[...]


def _modulation(c, w, b, tn):
    L, D, N = w.shape
    B = c.shape[0]
    return pl.pallas_call(
        _mod_kernel,
        grid=(L, N // tn),
        in_specs=[
            pl.BlockSpec((B, D), lambda l, j: (0, 0)),
            pl.BlockSpec((None, D, tn), lambda l, j: (l, 0, j)),
            pl.BlockSpec((None, 1, tn), lambda l, j: (l, 0, j)),
        ],
        out_specs=pl.BlockSpec((None, B, tn), lambda l, j: (l, 0, j)),
        out_shape=jax.ShapeDtypeStruct((L, B, N), F32),
        compiler_params=_cparams("parallel", "parallel"),
        name="adaln_mod",
    )(c, w, b.reshape(L, 1, N))


def _tables_kernel(pos_ref, invr_ref, invd_ref, *rest, n_cast):
    cast_src, (rc_ref, rs_ref, dc_ref, ds_ref), cast_dst = rest[:n_cast], rest[n_cast:n_cast + 4], rest[n_cast + 4:]
    _Casts.run(cast_src, cast_dst)
    p = pos_ref[...].astype(F32)
    ang = p * invr_ref[...]
    rc_ref[...] = jnp.cos(ang)
    rs_ref[...] = jnp.sin(ang)
    angd = p * invd_ref[...]
    rotary = jnp.bitwise_and(lax.broadcasted_iota(jnp.int32, angd.shape, 1), DIFF_HEAD_DIM - 1) < ROPE_DIM
    dc_ref[...] = jnp.where(rotary, jnp.cos(angd), 1.0)
    ds_ref[...] = jnp.where(rotary, jnp.sin(angd), 0.0)


def _rope_tables(positions, cast_sources):
    T = positions.size
    casts = _Casts(cast_sources, T // TM_TAB, lambda i: i)
    inv_r = RET_ROPE_BASE ** (-jnp.arange(0, RET_QK_DIM, 2, dtype=F32) / RET_QK_DIM)
    inv_d = ROPE_THETA ** (-jnp.arange(0, ROPE_DIM, 2, dtype=F32) / ROPE_DIM)
    pat = jnp.concatenate([inv_d, inv_d, jnp.zeros((DIFF_HEAD_DIM - ROPE_DIM,), F32)])
    inv_d_lanes = jnp.tile(pat, LANES // DIFF_HEAD_DIM)
    tab = jax.ShapeDtypeStruct((T, LANES), F32)
    tspec = pl.BlockSpec((TM_TAB, LANES), lambda i: (i, 0))
    vspec = pl.BlockSpec((1, LANES), lambda i: (0, 0))
    return pl.pallas_call(
        functools.partial(_tables_kernel, n_cast=casts.n),
        grid=(T // TM_TAB,),
        in_specs=[pl.BlockSpec((TM_TAB, 1), lambda i: (i, 0)), vspec, vspec] + casts.in_specs,
        out_specs=[tspec] * 4 + casts.out_specs,
        out_shape=[tab] * 4 + casts.out_shapes,
        compiler_params=_cparams("parallel"),
        name="rope_tables",
    )(positions.reshape(T, 1), inv_r.reshape(1, LANES), inv_d_lanes.reshape(1, LANES), *casts.arrays)


def _rotate_half_matrix():
    n = 2 * LANES
    half = ROPE_DIM // 2
    row = lax.broadcasted_iota(jnp.int32, (n, n), 0)
    col = lax.broadcasted_iota(jnp.int32, (n, n), 1)
    cm = jnp.bitwise_and(col, DIFF_HEAD_DIM - 1)
    return jnp.where((cm < half) & (row == col + half), -1.0,
                     jnp.where((cm >= half) & (cm < ROPE_DIM) & (row == col - half), 1.0, 0.0)).astype(BF16)


def _proj_kernel(x_ref, g_ref, sc_ref, sh_ref, w_ref, c_ref, s_ref, rot_ref, o_ref, *, rope_cols, scale):
    pair = 2 * LANES
    halves = [slice(r * SUB_PROJ, (r + 1) * SUB_PROJ) for r in range(TM_PROJ // SUB_PROJ)]
    hs = [((_rms(x_ref[rows, :]) * g_ref[...]) * (1.0 + sc_ref[...]) + sh_ref[...]).astype(BF16) for rows in halves]
    units = [(r, cc) for r in range(len(halves)) for cc in range(w_ref.shape[1] // TN_PROJ)]

    def main_dot(u):
        r, cc = units[u]
        return _dot(hs[r], w_ref[:, cc * TN_PROJ:(cc + 1) * TN_PROJ])

    acc_next = main_dot(0)
    for u, (r, cc) in enumerate(units):
        acc = acc_next
        if u + 1 < len(units):
            acc_next = main_dot(u + 1)
        rows = halves[r]
        for p in range(TN_PROJ // pair):
            xc = acc[:, p * pair:(p + 1) * pair]
            roped = cc * TN_PROJ + p * pair < rope_cols
            if roped:
                partner = _dot(xc.astype(BF16), rot_ref[...])
            for v in range(2):
                lanes = slice(v * LANES, (v + 1) * LANES)
                slab = (cc * TN_PROJ + p * pair) // LANES + v
                if roped:
                    o_ref[slab, rows, :] = ((xc[:, lanes] * c_ref[rows, :] + partner[:, lanes] * s_ref[rows, :])
                                            * scale).astype(BF16)
                else:
                    o_ref[slab, rows, :] = xc[:, lanes].astype(BF16)


def _proj(x, g, sc, sh, w, c_tab, s_tab, *, rope_cols, scale, batch, seq):
    T, D = x.shape
    N = w.shape[1]
    tpb = seq // TM_PROJ
    vec = pl.BlockSpec((1, D), lambda i: (0, 0))
    bvec = pl.BlockSpec((None, 1, D), lambda i: (i // tpb, 0, 0))
    tspec = pl.BlockSpec((TM_PROJ, LANES), lambda i: (i, 0))
    kern = functools.partial(_proj_kernel, rope_cols=rope_cols, scale=scale)
    return pl.pallas_call(
        kern,
        grid=(T // TM_PROJ,),
        in_specs=[
            pl.BlockSpec((TM_PROJ, D), lambda i: (i, 0)), vec, bvec, bvec,
            _resident(w), tspec, tspec,
            pl.BlockSpec((2 * LANES, 2 * LANES), lambda i: (0, 0)),
        ],
        out_specs=pl.BlockSpec((None, N // LANES, TM_PROJ, LANES), lambda i: (i // tpb, 0, i % tpb, 0)),
        out_shape=jax.ShapeDtypeStruct((batch, N // LANES, seq, LANES), BF16),
        compiler_params=_cparams("parallel"),
        name="proj_heads",
    )(x, g, sc, sh, w, c_tab, s_tab, _rotate_half_matrix())


def _ret_kernel(x_ref, g_ref, sc_ref, sh_ref, cos_ref, sin_ref, w_ref, *rest, n_cast):
    cast_src, z_ref, cast_dst = rest[:n_cast], rest[n_cast], rest[n_cast + 1:2 * n_cast + 1]
    r_ref, dm_ref, xi_ref, ze_ref = rest[2 * n_cast + 1:]
    _Casts.run(cast_src, cast_dst)
    C = RET_CHUNK
    n = pl.program_id(1)
    log_g = [math.log1p(-(2.0 ** (-5 - h))) for h in range(RET_HEADS)]

    @pl.when(n == 0)
    def _():
        r_ref[...] = jnp.zeros_like(r_ref)
        diff = (lax.broadcasted_iota(jnp.int32, (C, C), 0) - lax.broadcasted_iota(jnp.int32, (C, C), 1)).astype(F32)
        idx = lax.broadcasted_iota(jnp.int32, (C, RET_V_DIM), 0).astype(F32)
        for h in range(RET_HEADS):
            dm_ref[h] = jnp.where(diff >= 0, jnp.exp(jnp.maximum(diff, 0.0) * log_g[h]), 0.0)
            xi_ref[h] = jnp.exp((idx + 1.0) * log_g[h])
            ze_ref[h] = jnp.exp((C - 1.0 - idx) * log_g[h])

    for ci in range(RET_CHUNKS_PER_STEP):
        _ret_chunk(slice(ci * C, (ci + 1) * C), log_g, x_ref, g_ref, sc_ref, sh_ref, cos_ref, sin_ref, w_ref, z_ref,
                   r_ref, dm_ref, xi_ref, ze_ref)


def _ret_chunk(rows, log_g, x_ref, g_ref, sc_ref, sh_ref, cos_ref, sin_ref, w_ref, z_ref, r_ref, dm_ref, xi_ref, ze_ref):
    C = RET_CHUNK
    HK = RET_HEADS * RET_QK_DIM
    HV = RET_HEADS * RET_V_DIM
    hin = ((_rms(x_ref[rows, :]) * g_ref[...]) * (1.0 + sc_ref[...]) + sh_ref[...]).astype(BF16)
    cos = cos_ref[rows, :]
    sin = sin_ref[rows, :]

    def roped(col, mul):
        t = _dot(hin, w_ref[:, col:col + RET_QK_DIM])
        x1, x2 = t[:, :LANES], t[:, LANES:]
        return jnp.concatenate([(x1 * cos - x2 * sin) * mul, (x2 * cos + x1 * sin) * mul], axis=1).astype(BF16)

    for h in range(RET_HEADS):
        qh = roped(h * RET_QK_DIM, 1.0)
        kh = roped(HK + h * RET_QK_DIM, RET_QK_DIM ** -0.5)
        vh = _dot(hin, w_ref[:, 2 * HK + h * RET_V_DIM:2 * HK + (h + 1) * RET_V_DIM])
        gt = _dot(hin, w_ref[:, 2 * HK + HV + h * RET_V_DIM:2 * HK + HV + (h + 1) * RET_V_DIM])
        s = _dot_nt(qh, kh)
        rh = r_ref[h]
        cross = _dot(qh, rh.astype(BF16)) * xi_ref[h]
        r_ref[h] = rh * math.exp(C * log_g[h]) + _dot_tn(kh, (vh * ze_ref[h]).astype(BF16))
        intra = _dot((s * dm_ref[h]).astype(BF16), vh.astype(BF16))
        o = _rms(intra + cross)
        z_ref[rows, h * RET_V_DIM:(h + 1) * RET_V_DIM] = (gt * (1.0 / (1.0 + jnp.exp(-gt))) * o).astype(BF16)


def _retention_layer(x, g, sc, sh, cos, sin, w_in, cast_sources, *, batch, seq):
    T, D = x.shape
    C = RET_CHUNK
    rows = RET_CHUNKS_PER_STEP * C
    nch = seq // rows
    HV = RET_HEADS * RET_V_DIM
    row = lambda b, n: (b * nch + n, 0)
    vec = pl.BlockSpec((1, D), lambda b, n: (0, 0))
    bvec = pl.BlockSpec((None, 1, D), lambda b, n: (b, 0, 0))
    tspec = pl.BlockSpec((rows, LANES), row)
    casts = _Casts(cast_sources, batch * nch, lambda b, n: b * nch + n)
    return pl.pallas_call(
        functools.partial(_ret_kernel, n_cast=casts.n),
        grid=(batch, nch),
        in_specs=[pl.BlockSpec((rows, D), row), vec, bvec, bvec, tspec, tspec, _resident(w_in)] + casts.in_specs,
        out_specs=[pl.BlockSpec((rows, HV), row)] + casts.out_specs,
        out_shape=[jax.ShapeDtypeStruct((T, HV), BF16)] + casts.out_shapes,
        scratch_shapes=[
            pltpu.VMEM((RET_HEADS, RET_QK_DIM, RET_V_DIM), F32),
            pltpu.VMEM((RET_HEADS, C, C), F32),
            pltpu.VMEM((RET_HEADS, C, RET_V_DIM), F32),
            pltpu.VMEM((RET_HEADS, C, RET_V_DIM), F32),
        ],
        compiler_params=_cparams("arbitrary", "arbitrary"),
        name="retention",
    )(x, g, sc, sh, cos, sin, w_in, *casts.arrays)


def _attn_kernel(lam_ref, q_ref, k_ref, v_ref, sg_ref, z_ref, vt_ref, *, lambda_init):
    S = q_ref.shape[1]
    lf = lam_ref[...]
    lam = (jnp.exp(jnp.sum(lf[0:1] * lf[1:2], axis=-1, keepdims=True))
           - jnp.exp(jnp.sum(lf[2:3] * lf[3:4], axis=-1, keepdims=True)) + lambda_init)
    lane = lax.broadcasted_iota(jnp.int32, (TQ, LANES), 1)
    causal_t = lax.broadcasted_iota(jnp.int32, (TQ, TQ), 0) <= lax.broadcasted_iota(jnp.int32, (TQ, TQ), 1)
    for pr in range(ATTN_PAIRS):
        vt_ref[pr, :DIFF_V_DIM, :] = v_ref[pr].T
        vt_ref[pr, DIFF_V_DIM:, :] = jnp.ones((vt_ref.shape[1] - DIFF_V_DIM, S), BF16)

    def scores_t(unit):
        i, pr = unit
        n_keys = (i + 1) * TQ
        qp = q_ref[pr, i * TQ:(i + 1) * TQ, :]
        zero = jnp.zeros_like(qp)
        keys = k_ref[pr, :n_keys, :]
        return (_dot_nt(keys, jnp.where(lane < DIFF_HEAD_DIM, qp, zero)),
                _dot_nt(keys, jnp.where(lane < DIFF_HEAD_DIM, zero, qp)))

    def weighted_values_t(pr, st):
        n_main = st.shape[0] - TQ
        sd = jnp.where(causal_t, st[n_main:, :], -jnp.inf)
        m = jnp.max(sd, axis=0, keepdims=True)
        if n_main:
            m = jnp.maximum(m, jnp.max(st[:n_main, :], axis=0, keepdims=True))
        r = _dot(vt_ref[pr, :, n_main:n_main + TQ], jnp.exp2(sd - m).astype(BF16))
        if n_main:
            r = r + _dot(vt_ref[pr, :, :n_main], jnp.exp2(st[:n_main, :] - m).astype(BF16))
        return r

    units = [(i, pr) for i in range(S // TQ) for pr in range(ATTN_PAIRS)]
    ahead = [scores_t(u) for u in units[:ATTN_LOOKAHEAD]]
    for pos, (i, pr) in enumerate(units):
        n_keys = (i + 1) * TQ
        st1, st2 = ahead.pop(0)
        if pos + ATTN_LOOKAHEAD < len(units):
            ahead.append(scores_t(units[pos + ATTN_LOOKAHEAD]))
        r1 = weighted_values_t(pr, st1)
        r2 = weighted_values_t(pr, st2)
        ot = (r1[:DIFF_V_DIM] * (1.0 / r1[DIFF_V_DIM:DIFF_V_DIM + 1])
              - lam * (r2[:DIFF_V_DIM] * (1.0 / r2[DIFF_V_DIM:DIFF_V_DIM + 1])))
        z_ref[i * TQ:(i + 1) * TQ, pr * LANES:(pr + 1) * LANES] = (
            (_rms(ot.T) * sg_ref[...]) * (1.0 - lambda_init)).astype(BF16)


def _diff_attention(q, kv, lam, sg, *, lambda_init):
    B, H, S, _ = q.shape
    hp = H // ATTN_PAIRS
    slab = lambda off: pl.BlockSpec((None, ATTN_PAIRS, S, LANES), lambda b, p: (b, p + off, 0, 0))
    kern = functools.partial(_attn_kernel, lambda_init=lambda_init)
    return pl.pallas_call(
        kern,
        grid=(B, hp),
        in_specs=[
            pl.BlockSpec(lam.shape, lambda b, p: (0, 0)),
            slab(0), slab(0), slab(hp),
            pl.BlockSpec((1, DIFF_V_DIM), lambda b, p: (0, 0)),
        ],
        out_specs=pl.BlockSpec((S, ATTN_PAIRS * LANES), lambda b, p: (b, p)),
        out_shape=jax.ShapeDtypeStruct((B * S, H * LANES), BF16),
        scratch_shapes=[pltpu.VMEM((ATTN_PAIRS, DIFF_V_DIM + BF16_SUBLANES, S), BF16)],
        compiler_params=_cparams("parallel", "parallel"),
        name="diff_attention",
    )(lam, q, kv, kv, sg)


def _post_mlp_kernel(z_ref, wo_ref, x_ref, gaa_ref, g1_ref, g2_ref, scm_ref, shm_ref, w1_ref, w2_ref, gam_ref, g3_ref,
                     *rest, n_cast):
    cast_src, xo_ref, cast_dst, a_ref = rest[:n_cast], rest[n_cast], rest[n_cast + 1:2 * n_cast + 1], rest[-1]
    _Casts.run(cast_src, cast_dst)
    halves = [slice(r * SUB_MLP, (r + 1) * SUB_MLP) for r in range(TM_MLP // SUB_MLP)]
    ys = [_dot(z_ref[rows, :], wo_ref[...]) for rows in halves]
    hs = []
    for rows, y in zip(halves, ys):
        xn = x_ref[rows, :] + (1.0 + gaa_ref[...]) * (_rms(y) * g1_ref[...])
        xo_ref[rows, :] = xn
        hs.append(((_rms(xn) * g2_ref[...]) * (1.0 + scm_ref[...]) + shm_ref[...]).astype(BF16))
    for rows, h in zip(halves, hs):
        for f in range(a_ref.shape[1] // TF_MLP):
            cols = slice(f * TF_MLP, (f + 1) * TF_MLP)
            a = jnp.maximum(_dot(h, w1_ref[:, cols]), 0.0)
            a_ref[rows, cols] = (a * a).astype(BF16)
    y2s = [_dot(a_ref[rows, :], w2_ref[...]) for rows in halves]
    for rows, y2 in zip(halves, y2s):
        xo_ref[rows, :] = xo_ref[rows, :] + (1.0 + gam_ref[...]) * (_rms(y2) * g3_ref[...])


def _post_mlp(z, wo, x, gaa, g1, g2, scm, shm, w1, w2, gam, g3, cast_sources, *, seq):
    T, D = x.shape
    KZ = z.shape[1]
    F = w1.shape[1]
    tpb = seq // TM_MLP
    xspec = pl.BlockSpec((TM_MLP, D), lambda i: (i, 0))
    vec = pl.BlockSpec((1, D), lambda i: (0, 0))
    bvec = pl.BlockSpec((None, 1, D), lambda i: (i // tpb, 0, 0))
    casts = _Casts(cast_sources, T // TM_MLP, lambda i: i)
    return pl.pallas_call(
        functools.partial(_post_mlp_kernel, n_cast=casts.n),
        grid=(T // TM_MLP,),
        in_specs=[
            pl.BlockSpec((TM_MLP, KZ), lambda i: (i, 0)), _resident(wo), xspec,
            bvec, vec, vec, bvec, bvec,
            _resident(w1), _resident(w2), bvec, vec,
        ] + casts.in_specs,
        out_specs=[xspec] + casts.out_specs,
        out_shape=[jax.ShapeDtypeStruct((T, D), F32)] + casts.out_shapes,
        scratch_shapes=[pltpu.VMEM((TM_MLP, F), BF16)],
        compiler_params=_cparams("parallel"),
        name="post_mlp",
    )(z, wo, x, gaa, g1, g2, scm, shm, w1, w2, gam, g3, *casts.arrays)


def kernel(x, c, positions, norm_g, ada_w, ada_b, ret_w_in, ret_w_out, kv_norm_g, kv_ada_w, kv_ada_b, kv_w,
           diff_w_q, diff_w_o, diff_lam, diff_subln_g, mlp_w1, mlp_w2):
    B, S, D = x.shape
    T = B * S
    xf = x.reshape(T, D)

    mod = _modulation(c, ada_w, ada_b, 1536)
    kv_mod = _modulation(c, kv_ada_w[None], kv_ada_b[None], 1024)[0]

    def mvec(l, i):
        return mod[l, :, i * D:(i + 1) * D].reshape(B, 1, D)

    gvec = lambda l, i: norm_g[l, i].reshape(1, D)
    mlp_sources = lambda l: [(mlp_w1, l), (mlp_w2, l)] if l < DEPTH else []

    ret_cos, ret_sin, d_cos, d_sin, w_in, w_out, w1, w2 = _rope_tables(
        positions, [(ret_w_in, 0), (ret_w_out, 0)] + mlp_sources(0))
    w_kv = w_q = w_o = kv = None

    for l in range(DEPTH):
        sh_a, sc_a, ga_a, sh_m, sc_m, ga_m = (mvec(l, i) for i in range(6))
        if l == N_A:
            kv = _proj(xf, kv_norm_g.reshape(1, D), kv_mod[:, D:].reshape(B, 1, D), kv_mod[:, :D].reshape(B, 1, D),
                       w_kv, d_cos, d_sin, rope_cols=D, scale=1.0, batch=B, seq=S)
        if l < N_A:
            if l + 1 < N_A:
                nxt = [(ret_w_in, l + 1), (ret_w_out, l + 1)]
            else:
                nxt = [(kv_w[None], 0)] + [(diff_w_q, j) for j in range(DEPTH - N_A)] + [
                    (diff_w_o, j) for j in range(DEPTH - N_A)]
            z, *nxt_bf16 = _retention_layer(xf, gvec(l, 0), sc_a, sh_a, ret_cos, ret_sin, w_in, nxt, batch=B, seq=S)
            wo = w_out
            if l + 1 < N_A:
                w_in, w_out = nxt_bf16
            else:
                w_kv, w_q, w_o = nxt_bf16[0], nxt_bf16[1:1 + DEPTH - N_A], nxt_bf16[1 + DEPTH - N_A:]
        else:
            j = l - N_A
            q = _proj(xf, gvec(l, 0), sc_a, sh_a, w_q[j], d_cos, d_sin,
                      rope_cols=D, scale=DIFF_HEAD_DIM ** -0.5 * math.log2(math.e), batch=B, seq=S)
            z = _diff_attention(q, kv, diff_lam[j], diff_subln_g[j].reshape(1, DIFF_V_DIM),
                                lambda_init=0.8 - 0.6 * math.exp(-0.3 * l))
            wo = w_o[j]
        xf, *nxt_mlp = _post_mlp(z, wo, xf, ga_a, gvec(l, 1), gvec(l, 2), sc_m, sh_m, w1, w2, ga_m, gvec(l, 3),
                                 mlp_sources(l + 1), seq=S)
        if nxt_mlp:
            w1, w2 = nxt_mlp
    return xf.reshape(B, S, D)
```

```python
import functools
import math

import jax
import jax.numpy as jnp
from jax import lax
from jax.experimental import pallas as pl
from jax.experimental.pallas import tpu as pltpu

D_MODEL = 1024
DEPTH = 4
N_A = DEPTH // 2
RET_HEADS = 4
RET_QK_DIM = D_MODEL // RET_HEADS
RET_V_DIM = 2 * RET_QK_DIM
RET_ROPE_BASE = 10000.0
DIFF_HEAD_DIM = 64
DIFF_HEADS = D_MODEL // (2 * DIFF_HEAD_DIM)
DIFF_V_DIM = 2 * DIFF_HEAD_DIM
ROPE_THETA = 500000.0
ROPE_DIM = DIFF_HEAD_DIM // 4
D_FF = 4 * D_MODEL
EPS = 1e-6

LANES = 128
BF16_SUBLANES = 16
VMEM_LIMIT = 56 * 1024 * 1024

TM_PROJ = 1024
SUB_PROJ = 512
TN_PROJ = 512
TM_MLP = 512
SUB_MLP = 256
TF_MLP = 512
RET_CHUNK = 256
RET_CHUNKS_PER_STEP = 2
TQ = 256
ATTN_PAIRS = 2
ATTN_LOOKAHEAD = 4
TM_TAB = 1024

F32 = jnp.float32
BF16 = jnp.bfloat16


def _cparams(*sem):
    return pltpu.CompilerParams(dimension_semantics=sem, vmem_limit_bytes=VMEM_LIMIT)


def _resident(w):
    return pl.BlockSpec(w.shape, lambda *_: (0, 0), pipeline_mode=pl.Buffered(1))


class _Casts:
    def __init__(self, sources, steps, step_of):
        self.arrays = [stack for stack, _ in sources]
        self.in_specs, self.out_specs, self.out_shapes = [], [], []
        for stack, layer in sources:
            _, rows, cols = stack.shape
            blk = rows // steps
            self.in_specs.append(pl.BlockSpec((None, blk, cols), lambda *g, layer=layer: (layer, step_of(*g), 0)))
            self.out_specs.append(pl.BlockSpec((blk, cols), lambda *g: (step_of(*g), 0)))
            self.out_shapes.append(jax.ShapeDtypeStruct((rows, cols), BF16))
        self.n = len(sources)

    @staticmethod
    def run(src_refs, dst_refs):
        for src, dst in zip(src_refs, dst_refs):
            dst[...] = src[...].astype(BF16)


def _rms(x):
    return x * lax.rsqrt(jnp.mean(x * x, axis=-1, keepdims=True) + EPS)


def _dot(a, b):
    return jnp.dot(a, b, preferred_element_type=F32)


def _dot_nt(a, b):
    return lax.dot_general(a, b, (((1,), (1,)), ((), ())), preferred_element_type=F32)


def _dot_tn(a, b):
    return lax.dot_general(a, b, (((0,), (0,)), ((), ())), preferred_element_type=F32)


def _setup_kernel(pos_ref, invr_ref, invd_ref, c_ref, aw_ref, ab_ref, kw_ref, kb_ref, *rest, n_cast):
    cast_src, cast_dst = rest[:n_cast], rest[n_cast + 6:]
    rc_ref, rs_ref, dc_ref, ds_ref, mod_ref, kmod_ref = rest[n_cast:n_cast + 6]
    _Casts.run(cast_src, cast_dst)

    c = c_ref[...]
    c_act = (c * (1.0 / (1.0 + jnp.exp(-c)))).astype(BF16)
    for l in range(aw_ref.shape[0]):
        mod_ref[l] = _dot(c_act, aw_ref[l].astype(BF16)) + ab_ref[l]
    kmod_ref[...] = _dot(c_act, kw_ref[...].astype(BF16)) + kb_ref[...]

    p = pos_ref[...].astype(F32)
    ang = p * invr_ref[...]
    rc_ref[...] = jnp.cos(ang)
    rs_ref[...] = jnp.sin(ang)
    angd = p * invd_ref[...]
    rotary = jnp.bitwise_and(lax.broadcasted_iota(jnp.int32, angd.shape, 1), DIFF_HEAD_DIM - 1) < ROPE_DIM
    dc_ref[...] = jnp.where(rotary, jnp.cos(angd), 1.0)
    ds_ref[...] = jnp.where(rotary, jnp.sin(angd), 0.0)


def _setup(positions, c, ada_w, ada_b, kv_ada_w, kv_ada_b, cast_sources):
    T = positions.size
    steps = T // TM_TAB
    L, D, N = ada_w.shape
    NK = kv_ada_w.shape[1]
    B = c.shape[0]
    casts = _Casts(cast_sources, steps, lambda i: i)
    inv_r = RET_ROPE_BASE ** (-jnp.arange(0, RET_QK_DIM, 2, dtype=F32) / RET_QK_DIM)
    inv_d = ROPE_THETA ** (-jnp.arange(0, ROPE_DIM, 2, dtype=F32) / ROPE_DIM)
    pat = jnp.concatenate([inv_d, inv_d, jnp.zeros((DIFF_HEAD_DIM - ROPE_DIM,), F32)])
    inv_d_lanes = jnp.tile(pat, LANES // DIFF_HEAD_DIM)
    tab = jax.ShapeDtypeStruct((T, LANES), F32)
    tspec = pl.BlockSpec((TM_TAB, LANES), lambda i: (i, 0))
    vspec = pl.BlockSpec((1, LANES), lambda i: (0, 0))
    return pl.pallas_call(
        functools.partial(_setup_kernel, n_cast=casts.n),
        grid=(steps,),
        in_specs=[
            pl.BlockSpec((TM_TAB, 1), lambda i: (i, 0)), vspec, vspec,
            pl.BlockSpec((B, D), lambda i: (0, 0)),
            pl.BlockSpec((L, D, N // steps), lambda i: (0, 0, i)),
            pl.BlockSpec((L, 1, N // steps), lambda i: (0, 0, i)),
            pl.BlockSpec((D, NK // steps), lambda i: (0, i)),
            pl.BlockSpec((1, NK // steps), lambda i: (0, i)),
        ] + casts.in_specs,
        out_specs=[tspec] * 4 + [
            pl.BlockSpec((L, B, N // steps), lambda i: (0, 0, i)),
            pl.BlockSpec((B, NK // steps), lambda i: (0, i)),
        ] + casts.out_specs,
        out_shape=[tab] * 4 + [jax.ShapeDtypeStruct((L, B, N), F32), jax.ShapeDtypeStruct((B, NK), F32)]
        + casts.out_shapes,
        compiler_params=_cparams("parallel"),
        name="setup",
    )(positions.reshape(T, 1), inv_r.reshape(1, LANES), inv_d_lanes.reshape(1, LANES), c, ada_w,
      ada_b.reshape(L, 1, N), kv_ada_w, kv_ada_b.reshape(1, NK), *casts.arrays)


def _rotate_half_matrix():
    n = 2 * LANES
    half = ROPE_DIM // 2
    row = lax.broadcasted_iota(jnp.int32, (n, n), 0)
    col = lax.broadcasted_iota(jnp.int32, (n, n), 1)
    cm = jnp.bitwise_and(col, DIFF_HEAD_DIM - 1)
    return jnp.where((cm < half) & (row == col + half), -1.0,
                     jnp.where((cm >= half) & (cm < ROPE_DIM) & (row == col - half), 1.0, 0.0)).astype(BF16)


def _proj_kernel(x_ref, g_ref, sc_ref, sh_ref, w_ref, c_ref, s_ref, rot_ref, o_ref, *, rope_cols, scale):
    pair = 2 * LANES
    halves = [slice(r * SUB_PROJ, (r + 1) * SUB_PROJ) for r in range(TM_PROJ // SUB_PROJ)]
    hs = [((_rms(x_ref[rows, :]) * g_ref[...]) * (1.0 + sc_ref[...]) + sh_ref[...]).astype(BF16) for rows in halves]
    units = [(r, cc) for r in range(len(halves)) for cc in range(w_ref.shape[1] // TN_PROJ)]

    def main_dot(u):
        r, cc = units[u]
        return _dot(hs[r], w_ref[:, cc * TN_PROJ:(cc + 1) * TN_PROJ])

    acc_next = main_dot(0)
    for u, (r, cc) in enumerate(units):
        acc = acc_next
        if u + 1 < len(units):
            acc_next = main_dot(u + 1)
        rows = halves[r]
        for p in range(TN_PROJ // pair):
            xc = acc[:, p * pair:(p + 1) * pair]
            roped = cc * TN_PROJ + p * pair < rope_cols
            if roped:
                partner = _dot(xc.astype(BF16), rot_ref[...])
            for v in range(2):
                lanes = slice(v * LANES, (v + 1) * LANES)
                slab = (cc * TN_PROJ + p * pair) // LANES + v
                if roped:
                    o_ref[slab, rows, :] = ((xc[:, lanes] * c_ref[rows, :] + partner[:, lanes] * s_ref[rows, :])
                                            * scale).astype(BF16)
                else:
                    o_ref[slab, rows, :] = xc[:, lanes].astype(BF16)


def _proj(x, g, sc, sh, w, c_tab, s_tab, *, rope_cols, scale, batch, seq):
    T, D = x.shape
    N = w.shape[1]
    tpb = seq // TM_PROJ
    vec = pl.BlockSpec((1, D), lambda i: (0, 0))
    bvec = pl.BlockSpec((None, 1, D), lambda i: (i // tpb, 0, 0))
    tspec = pl.BlockSpec((TM_PROJ, LANES), lambda i: (i, 0))
    kern = functools.partial(_proj_kernel, rope_cols=rope_cols, scale=scale)
    return pl.pallas_call(
        kern,
        grid=(T // TM_PROJ,),
        in_specs=[
            pl.BlockSpec((TM_PROJ, D), lambda i: (i, 0)), vec, bvec, bvec,
            _resident(w), tspec, tspec,
            pl.BlockSpec((2 * LANES, 2 * LANES), lambda i: (0, 0)),
        ],
        out_specs=pl.BlockSpec((None, N // LANES, TM_PROJ, LANES), lambda i: (i // tpb, 0, i % tpb, 0)),
        out_shape=jax.ShapeDtypeStruct((batch, N // LANES, seq, LANES), BF16),
        compiler_params=_cparams("parallel"),
        name="proj_heads",
    )(x, g, sc, sh, w, c_tab, s_tab, _rotate_half_matrix())


def _ret_kernel(x_ref, g_ref, sc_ref, sh_ref, cos_ref, sin_ref, w_ref, *rest, n_cast):
    cast_src, z_ref, cast_dst = rest[:n_cast], rest[n_cast], rest[n_cast + 1:2 * n_cast + 1]
    r_ref, dm_ref, xi_ref, ze_ref = rest[2 * n_cast + 1:]
    _Casts.run(cast_src, cast_dst)
    C = RET_CHUNK
    n = pl.program_id(1)
    log_g = [math.log1p(-(2.0 ** (-5 - h))) for h in range(RET_HEADS)]

    @pl.when(n == 0)
    def _():
        r_ref[...] = jnp.zeros_like(r_ref)
        diff = (lax.broadcasted_iota(jnp.int32, (C, C), 0) - lax.broadcasted_iota(jnp.int32, (C, C), 1)).astype(F32)
        idx = lax.broadcasted_iota(jnp.int32, (C, RET_V_DIM), 0).astype(F32)
        for h in range(RET_HEADS):
            dm_ref[h] = jnp.where(diff >= 0, jnp.exp(jnp.maximum(diff, 0.0) * log_g[h]), 0.0)
            xi_ref[h] = jnp.exp((idx + 1.0) * log_g[h])
            ze_ref[h] = jnp.exp((C - 1.0 - idx) * log_g[h])

    for ci in range(RET_CHUNKS_PER_STEP):
        _ret_chunk(slice(ci * C, (ci + 1) * C), log_g, x_ref, g_ref, sc_ref, sh_ref, cos_ref, sin_ref, w_ref, z_ref,
                   r_ref, dm_ref, xi_ref, ze_ref)


def _ret_chunk(rows, log_g, x_ref, g_ref, sc_ref, sh_ref, cos_ref, sin_ref, w_ref, z_ref, r_ref, dm_ref, xi_ref, ze_ref):
    C = RET_CHUNK
    HK = RET_HEADS * RET_QK_DIM
    HV = RET_HEADS * RET_V_DIM
    hin = ((_rms(x_ref[rows, :]) * g_ref[...]) * (1.0 + sc_ref[...]) + sh_ref[...]).astype(BF16)
    cos = cos_ref[rows, :]
    sin = sin_ref[rows, :]

    def roped(col, mul):
        t = _dot(hin, w_ref[:, col:col + RET_QK_DIM])
        x1, x2 = t[:, :LANES], t[:, LANES:]
        return jnp.concatenate([(x1 * cos - x2 * sin) * mul, (x2 * cos + x1 * sin) * mul], axis=1).astype(BF16)

    for h in range(RET_HEADS):
        qh = roped(h * RET_QK_DIM, 1.0)
        kh = roped(HK + h * RET_QK_DIM, RET_QK_DIM ** -0.5)
        vh = _dot(hin, w_ref[:, 2 * HK + h * RET_V_DIM:2 * HK + (h + 1) * RET_V_DIM])
        gt = _dot(hin, w_ref[:, 2 * HK + HV + h * RET_V_DIM:2 * HK + HV + (h + 1) * RET_V_DIM])
        s = _dot_nt(qh, kh)
        rh = r_ref[h]
        cross = _dot(qh, rh.astype(BF16)) * xi_ref[h]
        r_ref[h] = rh * math.exp(C * log_g[h]) + _dot_tn(kh, (vh * ze_ref[h]).astype(BF16))
        intra = _dot((s * dm_ref[h]).astype(BF16), vh.astype(BF16))
        o = _rms(intra + cross)
        z_ref[rows, h * RET_V_DIM:(h + 1) * RET_V_DIM] = (gt * (1.0 / (1.0 + jnp.exp(-gt))) * o).astype(BF16)


def _retention_layer(x, g, sc, sh, cos, sin, w_in, cast_sources, *, batch, seq):
    T, D = x.shape
    C = RET_CHUNK
    rows = RET_CHUNKS_PER_STEP * C
    nch = seq // rows
    HV = RET_HEADS * RET_V_DIM
    row = lambda b, n: (b * nch + n, 0)
    vec = pl.BlockSpec((1, D), lambda b, n: (0, 0))
    bvec = pl.BlockSpec((None, 1, D), lambda b, n: (b, 0, 0))
    tspec = pl.BlockSpec((rows, LANES), row)
    casts = _Casts(cast_sources, batch * nch, lambda b, n: b * nch + n)
    return pl.pallas_call(
        functools.partial(_ret_kernel, n_cast=casts.n),
        grid=(batch, nch),
        in_specs=[pl.BlockSpec((rows, D), row), vec, bvec, bvec, tspec, tspec, _resident(w_in)] + casts.in_specs,
        out_specs=[pl.BlockSpec((rows, HV), row)] + casts.out_specs,
        out_shape=[jax.ShapeDtypeStruct((T, HV), BF16)] + casts.out_shapes,
        scratch_shapes=[
            pltpu.VMEM((RET_HEADS, RET_QK_DIM, RET_V_DIM), F32),
            pltpu.VMEM((RET_HEADS, C, C), F32),
            pltpu.VMEM((RET_HEADS, C, RET_V_DIM), F32),
            pltpu.VMEM((RET_HEADS, C, RET_V_DIM), F32),
        ],
        compiler_params=_cparams("arbitrary", "arbitrary"),
        name="retention",
    )(x, g, sc, sh, cos, sin, w_in, *casts.arrays)


def _attn_kernel(lam_ref, q_ref, k_ref, v_ref, sg_ref, z_ref, vt_ref, *, lambda_init):
    S = q_ref.shape[1]
    lf = lam_ref[...]
    lam = (jnp.exp(jnp.sum(lf[0:1] * lf[1:2], axis=-1, keepdims=True))
           - jnp.exp(jnp.sum(lf[2:3] * lf[3:4], axis=-1, keepdims=True)) + lambda_init)
    lane = lax.broadcasted_iota(jnp.int32, (TQ, LANES), 1)
    causal_t = lax.broadcasted_iota(jnp.int32, (TQ, TQ), 0) <= lax.broadcasted_iota(jnp.int32, (TQ, TQ), 1)
    for pr in range(ATTN_PAIRS):
        vt_ref[pr, :DIFF_V_DIM, :] = v_ref[pr].T
        vt_ref[pr, DIFF_V_DIM:, :] = jnp.ones((vt_ref.shape[1] - DIFF_V_DIM, S), BF16)

    def scores_t(unit):
        i, pr = unit
        n_keys = (i + 1) * TQ
        qp = q_ref[pr, i * TQ:(i + 1) * TQ, :]
        zero = jnp.zeros_like(qp)
        keys = k_ref[pr, :n_keys, :]
        return (_dot_nt(keys, jnp.where(lane < DIFF_HEAD_DIM, qp, zero)),
                _dot_nt(keys, jnp.where(lane < DIFF_HEAD_DIM, zero, qp)))

    def weighted_values_t(pr, st):
        n_main = st.shape[0] - TQ
        sd = jnp.where(causal_t, st[n_main:, :], -jnp.inf)
        m = jnp.max(sd, axis=0, keepdims=True)
        if n_main:
            m = jnp.maximum(m, jnp.max(st[:n_main, :], axis=0, keepdims=True))
        r = _dot(vt_ref[pr, :, n_main:n_main + TQ], jnp.exp2(sd - m).astype(BF16))
        if n_main:
            r = r + _dot(vt_ref[pr, :, :n_main], jnp.exp2(st[:n_main, :] - m).astype(BF16))
        return r

    units = [(i, pr) for i in range(S // TQ) for pr in range(ATTN_PAIRS)]
    ahead = [scores_t(u) for u in units[:ATTN_LOOKAHEAD]]
    for pos, (i, pr) in enumerate(units):
        n_keys = (i + 1) * TQ
        st1, st2 = ahead.pop(0)
        if pos + ATTN_LOOKAHEAD < len(units):
            ahead.append(scores_t(units[pos + ATTN_LOOKAHEAD]))
        r1 = weighted_values_t(pr, st1)
        r2 = weighted_values_t(pr, st2)
        ot = (r1[:DIFF_V_DIM] * (1.0 / r1[DIFF_V_DIM:DIFF_V_DIM + 1])
              - lam * (r2[:DIFF_V_DIM] * (1.0 / r2[DIFF_V_DIM:DIFF_V_DIM + 1])))
        z_ref[i * TQ:(i + 1) * TQ, pr * LANES:(pr + 1) * LANES] = (
            (_rms(ot.T) * sg_ref[...]) * (1.0 - lambda_init)).astype(BF16)


def _diff_attention(q, kv, lam, sg, *, lambda_init):
    B, H, S, _ = q.shape
    hp = H // ATTN_PAIRS
    slab = lambda off: pl.BlockSpec((None, ATTN_PAIRS, S, LANES), lambda b, p: (b, p + off, 0, 0))
    kern = functools.partial(_attn_kernel, lambda_init=lambda_init)
    return pl.pallas_call(
        kern,
        grid=(B, hp),
        in_specs=[
            pl.BlockSpec(lam.shape, lambda b, p: (0, 0)),
            slab(0), slab(0), slab(hp),
            pl.BlockSpec((1, DIFF_V_DIM), lambda b, p: (0, 0)),
        ],
        out_specs=pl.BlockSpec((S, ATTN_PAIRS * LANES), lambda b, p: (b, p)),
        out_shape=jax.ShapeDtypeStruct((B * S, H * LANES), BF16),
        scratch_shapes=[pltpu.VMEM((ATTN_PAIRS, DIFF_V_DIM + BF16_SUBLANES, S), BF16)],
        compiler_params=_cparams("parallel", "parallel"),
        name="diff_attention",
    )(lam, q, kv, kv, sg)


def _post_mlp_kernel(z_ref, wo_ref, x_ref, gaa_ref, g1_ref, g2_ref, scm_ref, shm_ref, w1_ref, w2_ref, gam_ref, g3_ref,
                     *rest, n_cast):
    cast_src, xo_ref, cast_dst, a_ref = rest[:n_cast], rest[n_cast], rest[n_cast + 1:2 * n_cast + 1], rest[-1]
    _Casts.run(cast_src, cast_dst)
    halves = [slice(r * SUB_MLP, (r + 1) * SUB_MLP) for r in range(TM_MLP // SUB_MLP)]
    ys = [_dot(z_ref[rows, :], wo_ref[...]) for rows in halves]
    hs = []
    for rows, y in zip(halves, ys):
        xn = x_ref[rows, :] + (1.0 + gaa_ref[...]) * (_rms(y) * g1_ref[...])
        xo_ref[rows, :] = xn
        hs.append(((_rms(xn) * g2_ref[...]) * (1.0 + scm_ref[...]) + shm_ref[...]).astype(BF16))
    for rows, h in zip(halves, hs):
        for f in range(a_ref.shape[1] // TF_MLP):
            cols = slice(f * TF_MLP, (f + 1) * TF_MLP)
            a = jnp.maximum(_dot(h, w1_ref[:, cols]), 0.0)
            a_ref[rows, cols] = (a * a).astype(BF16)
    y2s = [_dot(a_ref[rows, :], w2_ref[...]) for rows in halves]
    for rows, y2 in zip(halves, y2s):
        xo_ref[rows, :] = xo_ref[rows, :] + (1.0 + gam_ref[...]) * (_rms(y2) * g3_ref[...])


def _post_mlp(z, wo, x, gaa, g1, g2, scm, shm, w1, w2, gam, g3, cast_sources, *, seq):
    T, D = x.shape
    KZ = z.shape[1]
    F = w1.shape[1]
    tpb = seq // TM_MLP
    xspec = pl.BlockSpec((TM_MLP, D), lambda i: (i, 0))
    vec = pl.BlockSpec((1, D), lambda i: (0, 0))
    bvec = pl.BlockSpec((None, 1, D), lambda i: (i // tpb, 0, 0))
    casts = _Casts(cast_sources, T // TM_MLP, lambda i: i)
    return pl.pallas_call(
        functools.partial(_post_mlp_kernel, n_cast=casts.n),
        grid=(T // TM_MLP,),
        in_specs=[
            pl.BlockSpec((TM_MLP, KZ), lambda i: (i, 0)), _resident(wo), xspec,
            bvec, vec, vec, bvec, bvec,
            _resident(w1), _resident(w2), bvec, vec,
        ] + casts.in_specs,
        out_specs=[xspec] + casts.out_specs,
        out_shape=[jax.ShapeDtypeStruct((T, D), F32)] + casts.out_shapes,
        scratch_shapes=[pltpu.VMEM((TM_MLP, F), BF16)],
        compiler_params=_cparams("parallel"),
        name="post_mlp",
    )(z, wo, x, gaa, g1, g2, scm, shm, w1, w2, gam, g3, *casts.arrays)


def kernel(x, c, positions, norm_g, ada_w, ada_b, ret_w_in, ret_w_out, kv_norm_g, kv_ada_w, kv_ada_b, kv_w,
           diff_w_q, diff_w_o, diff_lam, diff_subln_g, mlp_w1, mlp_w2):
    B, S, D = x.shape
    T = B * S
    xf = x.reshape(T, D)

    gvec = lambda l, i: norm_g[l, i].reshape(1, D)
    mlp_sources = lambda l: [(mlp_w1, l), (mlp_w2, l)] if l < DEPTH else []

    ret_cos, ret_sin, d_cos, d_sin, mod, kv_mod, w_in, w_out = _setup(
        positions, c, ada_w, ada_b, kv_ada_w, kv_ada_b, [(ret_w_in, 0), (ret_w_out, 0)])
    w1 = w2 = w_kv = w_q = w_o = kv = None

    def mvec(l, i):
        return mod[l, :, i * D:(i + 1) * D].reshape(B, 1, D)

    for l in range(DEPTH):
        sh_a, sc_a, ga_a, sh_m, sc_m, ga_m = (mvec(l, i) for i in range(6))
        if l == N_A:
            kv = _proj(xf, kv_norm_g.reshape(1, D), kv_mod[:, D:].reshape(B, 1, D), kv_mod[:, :D].reshape(B, 1, D),
                       w_kv, d_cos, d_sin, rope_cols=D, scale=1.0, batch=B, seq=S)
        if l < N_A:
            if l + 1 < N_A:
                nxt = [(ret_w_in, l + 1), (ret_w_out, l + 1)]
            else:
                nxt = [(kv_w[None], 0)] + [(diff_w_q, j) for j in range(DEPTH - N_A)] + [
                    (diff_w_o, j) for j in range(DEPTH - N_A)]
            first_mlp = mlp_sources(0) if l == 0 else []
            z, *nxt_bf16 = _retention_layer(xf, gvec(l, 0), sc_a, sh_a, ret_cos, ret_sin, w_in, nxt + first_mlp,
                                            batch=B, seq=S)
            if first_mlp:
                w1, w2 = nxt_bf16[len(nxt):]
            wo = w_out
            if l + 1 < N_A:
                w_in, w_out = nxt_bf16[:len(nxt)]
            else:
                w_kv, w_q, w_o = nxt_bf16[0], nxt_bf16[1:1 + DEPTH - N_A], nxt_bf16[1 + DEPTH - N_A:len(nxt)]
        else:
            j = l - N_A
            q = _proj(xf, gvec(l, 0), sc_a, sh_a, w_q[j], d_cos, d_sin,
                      rope_cols=D, scale=DIFF_HEAD_DIM ** -0.5 * math.log2(math.e), batch=B, seq=S)
            z = _diff_attention(q, kv, diff_lam[j], diff_subln_g[j].reshape(1, DIFF_V_DIM),
                                lambda_init=0.8 - 0.6 * math.exp(-0.3 * l))
            wo = w_o[j]
        xf, *nxt_mlp = _post_mlp(z, wo, xf, ga_a, gvec(l, 1), gvec(l, 2), sc_m, sh_m, w1, w2, ga_m, gvec(l, 3),
                                 mlp_sources(l + 1), seq=S)
        if nxt_mlp:
            w1, w2 = nxt_mlp
    return xf.reshape(B, S, D)
```

```python
import functools
import math

import jax
import jax.numpy as jnp
from jax import lax
from jax.experimental import pallas as pl
from jax.experimental.pallas import tpu as pltpu

D_MODEL = 1024
DEPTH = 4
N_A = DEPTH // 2
RET_HEADS = 4
RET_QK_DIM = D_MODEL // RET_HEADS
RET_V_DIM = 2 * RET_QK_DIM
RET_ROPE_BASE = 10000.0
DIFF_HEAD_DIM = 64
DIFF_HEADS = D_MODEL // (2 * DIFF_HEAD_DIM)
DIFF_V_DIM = 2 * DIFF_HEAD_DIM
ROPE_THETA = 500000.0
ROPE_DIM = DIFF_HEAD_DIM // 4
D_FF = 4 * D_MODEL
EPS = 1e-6

LANES = 128
BF16_SUBLANES = 16
VMEM_LIMIT = 56 * 1024 * 1024

TM_PROJ = 1024
SUB_PROJ = 512
TN_PROJ = 512
TM_MLP = 512
SUBS_MLP = (256, 256)
TF_MLP = 512
RET_CHUNK = 256
RET_CHUNKS_PER_STEP = 2
TQ = 256
ATTN_PAIRS = 2
ATTN_LOOKAHEAD = 6
TM_TAB = 1024

F32 = jnp.float32
BF16 = jnp.bfloat16


def _cparams(*sem):
    return pltpu.CompilerParams(dimension_semantics=sem, vmem_limit_bytes=VMEM_LIMIT)


def _resident(w):
    return pl.BlockSpec(w.shape, lambda *_: (0, 0), pipeline_mode=pl.Buffered(1))


class _Casts:
    def __init__(self, sources, steps, step_of):
        self.arrays = [stack for stack, _ in sources]
        self.in_specs, self.out_specs, self.out_shapes = [], [], []
        for stack, layer in sources:
            _, rows, cols = stack.shape
            blk = rows // steps
            self.in_specs.append(pl.BlockSpec((None, blk, cols), lambda *g, layer=layer: (layer, step_of(*g), 0)))
            self.out_specs.append(pl.BlockSpec((blk, cols), lambda *g: (step_of(*g), 0)))
            self.out_shapes.append(jax.ShapeDtypeStruct((rows, cols), BF16))
        self.n = len(sources)

    @staticmethod
    def run(src_refs, dst_refs):
        for src, dst in zip(src_refs, dst_refs):
            dst[...] = src[...].astype(BF16)


def _rms(x):
    return x * lax.rsqrt(jnp.mean(x * x, axis=-1, keepdims=True) + EPS)


def _dot(a, b):
    return jnp.dot(a, b, preferred_element_type=F32)


def _dot_nt(a, b):
    return lax.dot_general(a, b, (((1,), (1,)), ((), ())), preferred_element_type=F32)


def _dot_tn(a, b):
    return lax.dot_general(a, b, (((0,), (0,)), ((), ())), preferred_element_type=F32)


def _setup_kernel(pos_ref, invr_ref, invd_ref, c_ref, aw_ref, ab_ref, kw_ref, kb_ref, *rest, n_cast):
    cast_src, cast_dst = rest[:n_cast], rest[n_cast + 6:]
    rc_ref, rs_ref, dc_ref, ds_ref, mod_ref, kmod_ref = rest[n_cast:n_cast + 6]
    _Casts.run(cast_src, cast_dst)

    c = c_ref[...]
    c_act = (c * (1.0 / (1.0 + jnp.exp(-c)))).astype(BF16)
    for l in range(aw_ref.shape[0]):
        mod_ref[l] = _dot(c_act, aw_ref[l].astype(BF16)) + ab_ref[l]
    kmod_ref[...] = _dot(c_act, kw_ref[...].astype(BF16)) + kb_ref[...]

    p = pos_ref[...].astype(F32)
    ang = p * invr_ref[...]
    rc_ref[...] = jnp.cos(ang)
    rs_ref[...] = jnp.sin(ang)
    angd = p * invd_ref[...]
    rotary = jnp.bitwise_and(lax.broadcasted_iota(jnp.int32, angd.shape, 1), DIFF_HEAD_DIM - 1) < ROPE_DIM
    dc_ref[...] = jnp.where(rotary, jnp.cos(angd), 1.0)
    ds_ref[...] = jnp.where(rotary, jnp.sin(angd), 0.0)


def _setup(positions, c, ada_w, ada_b, kv_ada_w, kv_ada_b, cast_sources):
    T = positions.size
    steps = T // TM_TAB
    L, D, N = ada_w.shape
    NK = kv_ada_w.shape[1]
    B = c.shape[0]
    casts = _Casts(cast_sources, steps, lambda i: i)
    inv_r = RET_ROPE_BASE ** (-jnp.arange(0, RET_QK_DIM, 2, dtype=F32) / RET_QK_DIM)
    inv_d = ROPE_THETA ** (-jnp.arange(0, ROPE_DIM, 2, dtype=F32) / ROPE_DIM)
    pat = jnp.concatenate([inv_d, inv_d, jnp.zeros((DIFF_HEAD_DIM - ROPE_DIM,), F32)])
    inv_d_lanes = jnp.tile(pat, LANES // DIFF_HEAD_DIM)
    tab = jax.ShapeDtypeStruct((T, LANES), F32)
    tspec = pl.BlockSpec((TM_TAB, LANES), lambda i: (i, 0))
    vspec = pl.BlockSpec((1, LANES), lambda i: (0, 0))
    return pl.pallas_call(
        functools.partial(_setup_kernel, n_cast=casts.n),
        grid=(steps,),
        in_specs=[
            pl.BlockSpec((TM_TAB, 1), lambda i: (i, 0)), vspec, vspec,
            pl.BlockSpec((B, D), lambda i: (0, 0)),
            pl.BlockSpec((L, D, N // steps), lambda i: (0, 0, i)),
            pl.BlockSpec((L, 1, N // steps), lambda i: (0, 0, i)),
            pl.BlockSpec((D, NK // steps), lambda i: (0, i)),
            pl.BlockSpec((1, NK // steps), lambda i: (0, i)),
        ] + casts.in_specs,
        out_specs=[tspec] * 4 + [
            pl.BlockSpec((L, B, N // steps), lambda i: (0, 0, i)),
            pl.BlockSpec((B, NK // steps), lambda i: (0, i)),
        ] + casts.out_specs,
        out_shape=[tab] * 4 + [jax.ShapeDtypeStruct((L, B, N), F32), jax.ShapeDtypeStruct((B, NK), F32)]
        + casts.out_shapes,
        compiler_params=_cparams("parallel"),
        name="setup",
    )(positions.reshape(T, 1), inv_r.reshape(1, LANES), inv_d_lanes.reshape(1, LANES), c, ada_w,
      ada_b.reshape(L, 1, N), kv_ada_w, kv_ada_b.reshape(1, NK), *casts.arrays)


def _rotate_half_matrix():
    n = 2 * LANES
    half = ROPE_DIM // 2
    row = lax.broadcasted_iota(jnp.int32, (n, n), 0)
    col = lax.broadcasted_iota(jnp.int32, (n, n), 1)
    cm = jnp.bitwise_and(col, DIFF_HEAD_DIM - 1)
    return jnp.where((cm < half) & (row == col + half), -1.0,
                     jnp.where((cm >= half) & (cm < ROPE_DIM) & (row == col - half), 1.0, 0.0)).astype(BF16)


def _proj_kernel(x_ref, g_ref, sc_ref, sh_ref, w_ref, c_ref, s_ref, rot_ref, o_ref, *, rope_cols, scale):
    pair = 2 * LANES
    halves = [slice(r * SUB_PROJ, (r + 1) * SUB_PROJ) for r in range(TM_PROJ // SUB_PROJ)]
    hs = [((_rms(x_ref[rows, :]) * g_ref[...]) * (1.0 + sc_ref[...]) + sh_ref[...]).astype(BF16) for rows in halves]
    units = [(r, cc) for r in range(len(halves)) for cc in range(w_ref.shape[1] // TN_PROJ)]

    def main_dot(u):
        r, cc = units[u]
        return _dot(hs[r], w_ref[:, cc * TN_PROJ:(cc + 1) * TN_PROJ])

    acc_next = main_dot(0)
    for u, (r, cc) in enumerate(units):
        acc = acc_next
        if u + 1 < len(units):
            acc_next = main_dot(u + 1)
        rows = halves[r]
        for p in range(TN_PROJ // pair):
            xc = acc[:, p * pair:(p + 1) * pair]
            roped = cc * TN_PROJ + p * pair < rope_cols
            if roped:
                partner = _dot(xc.astype(BF16), rot_ref[...])
            for v in range(2):
                lanes = slice(v * LANES, (v + 1) * LANES)
                slab = (cc * TN_PROJ + p * pair) // LANES + v
                if roped:
                    o_ref[slab, rows, :] = ((xc[:, lanes] * c_ref[rows, :] + partner[:, lanes] * s_ref[rows, :])
                                            * scale).astype(BF16)
                else:
                    o_ref[slab, rows, :] = xc[:, lanes].astype(BF16)


def _proj(x, g, sc, sh, w, c_tab, s_tab, *, rope_cols, scale, batch, seq):
    T, D = x.shape
    N = w.shape[1]
    tpb = seq // TM_PROJ
    vec = pl.BlockSpec((1, D), lambda i: (0, 0))
    bvec = pl.BlockSpec((None, 1, D), lambda i: (i // tpb, 0, 0))
    tspec = pl.BlockSpec((TM_PROJ, LANES), lambda i: (i, 0))
    kern = functools.partial(_proj_kernel, rope_cols=rope_cols, scale=scale)
    return pl.pallas_call(
        kern,
        grid=(T // TM_PROJ,),
        in_specs=[
            pl.BlockSpec((TM_PROJ, D), lambda i: (i, 0)), vec, bvec, bvec,
            _resident(w), tspec, tspec,
            pl.BlockSpec((2 * LANES, 2 * LANES), lambda i: (0, 0)),
        ],
        out_specs=pl.BlockSpec((None, N // LANES, TM_PROJ, LANES), lambda i: (i // tpb, 0, i % tpb, 0)),
        out_shape=jax.ShapeDtypeStruct((batch, N // LANES, seq, LANES), BF16),
        compiler_params=_cparams("parallel"),
        name="proj_heads",
    )(x, g, sc, sh, w, c_tab, s_tab, _rotate_half_matrix())


def _ret_kernel(x_ref, g_ref, sc_ref, sh_ref, cos_ref, sin_ref, w_ref, *rest, n_cast):
    cast_src, z_ref, cast_dst = rest[:n_cast], rest[n_cast], rest[n_cast + 1:2 * n_cast + 1]
    r_ref, dm_ref, xi_ref, ze_ref = rest[2 * n_cast + 1:]
    _Casts.run(cast_src, cast_dst)
    C = RET_CHUNK
    b, n = pl.program_id(0), pl.program_id(1)
    log_g = [math.log1p(-(2.0 ** (-5 - h))) for h in range(RET_HEADS)]

    @pl.when(n == 0)
    def _():
        r_ref[...] = jnp.zeros_like(r_ref)

    @pl.when((b == 0) & (n == 0))
    def _():
        diff = (lax.broadcasted_iota(jnp.int32, (C, C), 0) - lax.broadcasted_iota(jnp.int32, (C, C), 1)).astype(F32)
        idx = lax.broadcasted_iota(jnp.int32, (C, RET_V_DIM), 0).astype(F32)
        for h in range(RET_HEADS):
            dm_ref[h] = jnp.where(diff >= 0, jnp.exp(jnp.maximum(diff, 0.0) * log_g[h]), 0.0)
            xi_ref[h] = jnp.exp((idx + 1.0) * log_g[h])
            ze_ref[h] = jnp.exp((C - 1.0 - idx) * log_g[h])

    for ci in range(RET_CHUNKS_PER_STEP):
        _ret_chunk(slice(ci * C, (ci + 1) * C), log_g, x_ref, g_ref, sc_ref, sh_ref, cos_ref, sin_ref, w_ref, z_ref,
                   r_ref, dm_ref, xi_ref, ze_ref)


def _ret_chunk(rows, log_g, x_ref, g_ref, sc_ref, sh_ref, cos_ref, sin_ref, w_ref, z_ref, r_ref, dm_ref, xi_ref, ze_ref):
    C = RET_CHUNK
    HK = RET_HEADS * RET_QK_DIM
    HV = RET_HEADS * RET_V_DIM
    hin = ((_rms(x_ref[rows, :]) * g_ref[...]) * (1.0 + sc_ref[...]) + sh_ref[...]).astype(BF16)
    cos = cos_ref[rows, :]
    sin = sin_ref[rows, :]

    def roped(col, mul):
        t = _dot(hin, w_ref[:, col:col + RET_QK_DIM])
        x1, x2 = t[:, :LANES], t[:, LANES:]
        return jnp.concatenate([(x1 * cos - x2 * sin) * mul, (x2 * cos + x1 * sin) * mul], axis=1).astype(BF16)

    for h in range(RET_HEADS):
        qh = roped(h * RET_QK_DIM, 1.0)
        kh = roped(HK + h * RET_QK_DIM, RET_QK_DIM ** -0.5)
        vh = _dot(hin, w_ref[:, 2 * HK + h * RET_V_DIM:2 * HK + (h + 1) * RET_V_DIM])
        gt = _dot(hin, w_ref[:, 2 * HK + HV + h * RET_V_DIM:2 * HK + HV + (h + 1) * RET_V_DIM])
        s = _dot_nt(qh, kh)
        rh = r_ref[h]
        cross = _dot(qh, rh.astype(BF16)) * xi_ref[h]
        r_ref[h] = rh * math.exp(C * log_g[h]) + _dot_tn(kh, (vh * ze_ref[h]).astype(BF16))
        intra = _dot((s * dm_ref[h]).astype(BF16), vh.astype(BF16))
        o = _rms(intra + cross)
        z_ref[rows, h * RET_V_DIM:(h + 1) * RET_V_DIM] = (gt * (1.0 / (1.0 + jnp.exp(-gt))) * o).astype(BF16)


def _retention_layer(x, g, sc, sh, cos, sin, w_in, cast_sources, *, batch, seq):
    T, D = x.shape
    C = RET_CHUNK
    rows = RET_CHUNKS_PER_STEP * C
    nch = seq // rows
    HV = RET_HEADS * RET_V_DIM
    row = lambda b, n: (b * nch + n, 0)
    vec = pl.BlockSpec((1, D), lambda b, n: (0, 0))
    bvec = pl.BlockSpec((None, 1, D), lambda b, n: (b, 0, 0))
    tspec = pl.BlockSpec((rows, LANES), row)
    casts = _Casts(cast_sources, batch * nch, lambda b, n: b * nch + n)
    return pl.pallas_call(
        functools.partial(_ret_kernel, n_cast=casts.n),
        grid=(batch, nch),
        in_specs=[pl.BlockSpec((rows, D), row), vec, bvec, bvec, tspec, tspec, _resident(w_in)] + casts.in_specs,
        out_specs=[pl.BlockSpec((rows, HV), row)] + casts.out_specs,
        out_shape=[jax.ShapeDtypeStruct((T, HV), BF16)] + casts.out_shapes,
        scratch_shapes=[
            pltpu.VMEM((RET_HEADS, RET_QK_DIM, RET_V_DIM), F32),
            pltpu.VMEM((RET_HEADS, C, C), F32),
            pltpu.VMEM((RET_HEADS, C, RET_V_DIM), F32),
            pltpu.VMEM((RET_HEADS, C, RET_V_DIM), F32),
        ],
        compiler_params=_cparams("arbitrary", "arbitrary"),
        name="retention",
    )(x, g, sc, sh, cos, sin, w_in, *casts.arrays)


def _attn_kernel(lam_ref, q_ref, k_ref, v_ref, sg_ref, z_ref, vt_ref, *, lambda_init):
    S = q_ref.shape[1]
    lf = lam_ref[...]
    lam = (jnp.exp(jnp.sum(lf[0:1] * lf[1:2], axis=-1, keepdims=True))
           - jnp.exp(jnp.sum(lf[2:3] * lf[3:4], axis=-1, keepdims=True)) + lambda_init)
    lane = lax.broadcasted_iota(jnp.int32, (TQ, LANES), 1)
    causal_t = lax.broadcasted_iota(jnp.int32, (TQ, TQ), 0) <= lax.broadcasted_iota(jnp.int32, (TQ, TQ), 1)
    for pr in range(ATTN_PAIRS):
        vt_ref[pr, :DIFF_V_DIM, :] = v_ref[pr].T
        vt_ref[pr, DIFF_V_DIM:, :] = jnp.ones((vt_ref.shape[1] - DIFF_V_DIM, S), BF16)

    def scores_t(unit):
        i, pr = unit
        n_keys = (i + 1) * TQ
        qp = q_ref[pr, i * TQ:(i + 1) * TQ, :]
        zero = jnp.zeros_like(qp)
        keys = k_ref[pr, :n_keys, :]
        return (_dot_nt(keys, jnp.where(lane < DIFF_HEAD_DIM, qp, zero)),
                _dot_nt(keys, jnp.where(lane < DIFF_HEAD_DIM, zero, qp)))

    def weighted_values_t(pr, st):
        n_main = st.shape[0] - TQ
        sd = jnp.where(causal_t, st[n_main:, :], -jnp.inf)
        m = jnp.max(sd, axis=0, keepdims=True)
        if n_main:
            m = jnp.maximum(m, jnp.max(st[:n_main, :], axis=0, keepdims=True))
        r = _dot(vt_ref[pr, :, n_main:n_main + TQ], jnp.exp2(sd - m).astype(BF16))
        if n_main:
            r = r + _dot(vt_ref[pr, :, :n_main], jnp.exp2(st[:n_main, :] - m).astype(BF16))
        return r

    units = [(i, pr) for i in range(S // TQ) for pr in range(ATTN_PAIRS)]
    ahead = [scores_t(u) for u in units[:ATTN_LOOKAHEAD]]
    for pos, (i, pr) in enumerate(units):
        n_keys = (i + 1) * TQ
        st1, st2 = ahead.pop(0)
        if pos + ATTN_LOOKAHEAD < len(units):
            ahead.append(scores_t(units[pos + ATTN_LOOKAHEAD]))
        r1 = weighted_values_t(pr, st1)
        r2 = weighted_values_t(pr, st2)
        ot = (r1[:DIFF_V_DIM] * (1.0 / r1[DIFF_V_DIM:DIFF_V_DIM + 1])
              - lam * (r2[:DIFF_V_DIM] * (1.0 / r2[DIFF_V_DIM:DIFF_V_DIM + 1])))
        z_ref[i * TQ:(i + 1) * TQ, pr * LANES:(pr + 1) * LANES] = (
            (_rms(ot.T) * sg_ref[...]) * (1.0 - lambda_init)).astype(BF16)


def _diff_attention(q, kv, lam, sg, *, lambda_init):
    B, H, S, _ = q.shape
    hp = H // ATTN_PAIRS
    slab = lambda off: pl.BlockSpec((None, ATTN_PAIRS, S, LANES), lambda b, p: (b, p + off, 0, 0))
    kern = functools.partial(_attn_kernel, lambda_init=lambda_init)
    return pl.pallas_call(
        kern,
        grid=(B, hp),
        in_specs=[
            pl.BlockSpec(lam.shape, lambda b, p: (0, 0)),
            slab(0), slab(0), slab(hp),
            pl.BlockSpec((1, DIFF_V_DIM), lambda b, p: (0, 0)),
        ],
        out_specs=pl.BlockSpec((S, ATTN_PAIRS * LANES), lambda b, p: (b, p)),
        out_shape=jax.ShapeDtypeStruct((B * S, H * LANES), BF16),
        scratch_shapes=[pltpu.VMEM((ATTN_PAIRS, DIFF_V_DIM + BF16_SUBLANES, S), BF16)],
        compiler_params=_cparams("parallel", "parallel"),
        name="diff_attention",
    )(lam, q, kv, kv, sg)


def _post_mlp_kernel(z_ref, wo_ref, x_ref, gaa_ref, g1_ref, g2_ref, scm_ref, shm_ref, w1_ref, w2_ref, gam_ref, g3_ref,
                     *rest, n_cast):
    cast_src, xo_ref, cast_dst, a_ref = rest[:n_cast], rest[n_cast], rest[n_cast + 1:2 * n_cast + 1], rest[-1]
    _Casts.run(cast_src, cast_dst)
    starts = [sum(SUBS_MLP[:r]) for r in range(len(SUBS_MLP))]
    halves = [slice(s0, s0 + n) for s0, n in zip(starts, SUBS_MLP)]
    ys = [_dot(z_ref[rows, :], wo_ref[...]) for rows in halves]
    hs = []
    for rows, y in zip(halves, ys):
        xn = x_ref[rows, :] + (1.0 + gaa_ref[...]) * (_rms(y) * g1_ref[...])
        xo_ref[rows, :] = xn
        hs.append(((_rms(xn) * g2_ref[...]) * (1.0 + scm_ref[...]) + shm_ref[...]).astype(BF16))
    for rows, h in zip(halves, hs):
        for f in range(a_ref.shape[1] // TF_MLP):
            cols = slice(f * TF_MLP, (f + 1) * TF_MLP)
            a = jnp.maximum(_dot(h, w1_ref[:, cols]), 0.0)
            a_ref[rows, cols] = (a * a).astype(BF16)
    y2s = [_dot(a_ref[rows, :], w2_ref[...]) for rows in halves]
    for rows, y2 in zip(halves, y2s):
        xo_ref[rows, :] = xo_ref[rows, :] + (1.0 + gam_ref[...]) * (_rms(y2) * g3_ref[...])


def _post_mlp(z, wo, x, gaa, g1, g2, scm, shm, w1, w2, gam, g3, cast_sources, *, seq):
    T, D = x.shape
    KZ = z.shape[1]
    F = w1.shape[1]
    tpb = seq // TM_MLP
    xspec = pl.BlockSpec((TM_MLP, D), lambda i: (i, 0))
    vec = pl.BlockSpec((1, D), lambda i: (0, 0))
    bvec = pl.BlockSpec((None, 1, D), lambda i: (i // tpb, 0, 0))
    casts = _Casts(cast_sources, T // TM_MLP, lambda i: i)
    return pl.pallas_call(
        functools.partial(_post_mlp_kernel, n_cast=casts.n),
        grid=(T // TM_MLP,),
        in_specs=[
            pl.BlockSpec((TM_MLP, KZ), lambda i: (i, 0)), _resident(wo), xspec,
            bvec, vec, vec, bvec, bvec,
            _resident(w1), _resident(w2), bvec, vec,
        ] + casts.in_specs,
        out_specs=[xspec] + casts.out_specs,
        out_shape=[jax.ShapeDtypeStruct((T, D), F32)] + casts.out_shapes,
        scratch_shapes=[pltpu.VMEM((TM_MLP, F), BF16)],
        compiler_params=_cparams("parallel"),
        name="post_mlp",
    )(z, wo, x, gaa, g1, g2, scm, shm, w1, w2, gam, g3, *casts.arrays)


def kernel(x, c, positions, norm_g, ada_w, ada_b, ret_w_in, ret_w_out, kv_norm_g, kv_ada_w, kv_ada_b, kv_w,
           diff_w_q, diff_w_o, diff_lam, diff_subln_g, mlp_w1, mlp_w2):
    B, S, D = x.shape
    T = B * S
    xf = x.reshape(T, D)

    gvec = lambda l, i: norm_g[l, i].reshape(1, D)
    mlp_sources = lambda l: [(mlp_w1, l), (mlp_w2, l)] if l < DEPTH else []

    ret_cos, ret_sin, d_cos, d_sin, mod, kv_mod, w_in, w_out = _setup(
        positions, c, ada_w, ada_b, kv_ada_w, kv_ada_b, [(ret_w_in, 0), (ret_w_out, 0)])
    w1 = w2 = w_kv = w_q = w_o = kv = None

    def mvec(l, i):
        return mod[l, :, i * D:(i + 1) * D].reshape(B, 1, D)

    for l in range(DEPTH):
        sh_a, sc_a, ga_a, sh_m, sc_m, ga_m = (mvec(l, i) for i in range(6))
        if l == N_A:
            kv = _proj(xf, kv_norm_g.reshape(1, D), kv_mod[:, D:].reshape(B, 1, D), kv_mod[:, :D].reshape(B, 1, D),
                       w_kv, d_cos, d_sin, rope_cols=D, scale=1.0, batch=B, seq=S)
        if l < N_A:
            if l + 1 < N_A:
                nxt = [(ret_w_in, l + 1), (ret_w_out, l + 1)]
            else:
                nxt = [(kv_w[None], 0)] + [(diff_w_q, j) for j in range(DEPTH - N_A)] + [
                    (diff_w_o, j) for j in range(DEPTH - N_A)]
            first_mlp = mlp_sources(0) if l == 0 else []
            z, *nxt_bf16 = _retention_layer(xf, gvec(l, 0), sc_a, sh_a, ret_cos, ret_sin, w_in, nxt + first_mlp,
                                            batch=B, seq=S)
            if first_mlp:
                w1, w2 = nxt_bf16[len(nxt):]
            wo = w_out
            if l + 1 < N_A:
                w_in, w_out = nxt_bf16[:len(nxt)]
            else:
                w_kv, w_q, w_o = nxt_bf16[0], nxt_bf16[1:1 + DEPTH - N_A], nxt_bf16[1 + DEPTH - N_A:len(nxt)]
        else:
            j = l - N_A
            q = _proj(xf, gvec(l, 0), sc_a, sh_a, w_q[j], d_cos, d_sin,
                      rope_cols=D, scale=DIFF_HEAD_DIM ** -0.5 * math.log2(math.e), batch=B, seq=S)
            z = _diff_attention(q, kv, diff_lam[j], diff_subln_g[j].reshape(1, DIFF_V_DIM),
                                lambda_init=0.8 - 0.6 * math.exp(-0.3 * l))
            wo = w_o[j]
        xf, *nxt_mlp = _post_mlp(z, wo, xf, ga_a, gvec(l, 1), gvec(l, 2), sc_m, sh_m, w1, w2, ga_m, gvec(l, 3),
                                 mlp_sources(l + 1), seq=S)
        if nxt_mlp:
            w1, w2 = nxt_mlp
    return xf.reshape(B, S, D)
```

```python
import functools
import math

import jax
import jax.numpy as jnp
from jax import lax
from jax.experimental import pallas as pl
from jax.experimental.pallas import tpu as pltpu

D_MODEL = 1024
DEPTH = 4
N_A = DEPTH // 2
RET_HEADS = 4
RET_QK_DIM = D_MODEL // RET_HEADS
RET_V_DIM = 2 * RET_QK_DIM
RET_ROPE_BASE = 10000.0
DIFF_HEAD_DIM = 64
DIFF_HEADS = D_MODEL // (2 * DIFF_HEAD_DIM)
DIFF_V_DIM = 2 * DIFF_HEAD_DIM
ROPE_THETA = 500000.0
ROPE_DIM = DIFF_HEAD_DIM // 4
D_FF = 4 * D_MODEL
EPS = 1e-6

LANES = 128
BF16_SUBLANES = 16
VMEM_LIMIT = 56 * 1024 * 1024

TM_PROJ = 2048
SUB_PROJ = 512
TN_PROJ = 512
TM_MLP = 512
SUBS_MLP = (256, 256)
TF_MLP = 512
RET_CHUNK = 256
RET_CHUNKS_PER_STEP = 2
TQ = 256
ATTN_PAIRS = 2
ATTN_LOOKAHEAD = 6
TM_TAB = 1024

F32 = jnp.float32
BF16 = jnp.bfloat16


def _cparams(*sem):
    return pltpu.CompilerParams(dimension_semantics=sem, vmem_limit_bytes=VMEM_LIMIT)


def _resident(w):
    return pl.BlockSpec(w.shape, lambda *_: (0, 0), pipeline_mode=pl.Buffered(1))


class _Casts:
    def __init__(self, sources, steps, step_of):
        self.arrays = [stack for stack, _ in sources]
        self.in_specs, self.out_specs, self.out_shapes = [], [], []
        for stack, layer in sources:
            _, rows, cols = stack.shape
            blk = rows // steps
            self.in_specs.append(pl.BlockSpec((None, blk, cols), lambda *g, layer=layer: (layer, step_of(*g), 0)))
            self.out_specs.append(pl.BlockSpec((blk, cols), lambda *g: (step_of(*g), 0)))
            self.out_shapes.append(jax.ShapeDtypeStruct((rows, cols), BF16))
        self.n = len(sources)

    @staticmethod
    def run(src_refs, dst_refs):
        for src, dst in zip(src_refs, dst_refs):
            dst[...] = src[...].astype(BF16)


def _rms(x):
    return x * lax.rsqrt(jnp.mean(x * x, axis=-1, keepdims=True) + EPS)


def _dot(a, b):
    return jnp.dot(a, b, preferred_element_type=F32)


def _dot_nt(a, b):
    return lax.dot_general(a, b, (((1,), (1,)), ((), ())), preferred_element_type=F32)


def _dot_tn(a, b):
    return lax.dot_general(a, b, (((0,), (0,)), ((), ())), preferred_element_type=F32)


def _setup_kernel(pos_ref, invr_ref, invd_ref, c_ref, aw_ref, ab_ref, kw_ref, kb_ref, *rest, n_cast):
    cast_src, cast_dst = rest[:n_cast], rest[n_cast + 6:]
    rc_ref, rs_ref, dc_ref, ds_ref, mod_ref, kmod_ref = rest[n_cast:n_cast + 6]
    _Casts.run(cast_src, cast_dst)

    c = c_ref[...]
    c_act = (c * (1.0 / (1.0 + jnp.exp(-c)))).astype(BF16)
    for l in range(aw_ref.shape[0]):
        mod_ref[l] = _dot(c_act, aw_ref[l].astype(BF16)) + ab_ref[l]
    kmod_ref[...] = _dot(c_act, kw_ref[...].astype(BF16)) + kb_ref[...]

    p = pos_ref[...].astype(F32)
    ang = p * invr_ref[...]
    rc_ref[...] = jnp.cos(ang)
    rs_ref[...] = jnp.sin(ang)
    angd = p * invd_ref[...]
    rotary = jnp.bitwise_and(lax.broadcasted_iota(jnp.int32, angd.shape, 1), DIFF_HEAD_DIM - 1) < ROPE_DIM
    dc_ref[...] = jnp.where(rotary, jnp.cos(angd), 1.0)
    ds_ref[...] = jnp.where(rotary, jnp.sin(angd), 0.0)


def _setup(positions, c, ada_w, ada_b, kv_ada_w, kv_ada_b, cast_sources):
    T = positions.size
    steps = T // TM_TAB
    L, D, N = ada_w.shape
    NK = kv_ada_w.shape[1]
    B = c.shape[0]
    casts = _Casts(cast_sources, steps, lambda i: i)
    inv_r = RET_ROPE_BASE ** (-jnp.arange(0, RET_QK_DIM, 2, dtype=F32) / RET_QK_DIM)
    inv_d = ROPE_THETA ** (-jnp.arange(0, ROPE_DIM, 2, dtype=F32) / ROPE_DIM)
    pat = jnp.concatenate([inv_d, inv_d, jnp.zeros((DIFF_HEAD_DIM - ROPE_DIM,), F32)])
    inv_d_lanes = jnp.tile(pat, LANES // DIFF_HEAD_DIM)
    tab = jax.ShapeDtypeStruct((T, LANES), F32)
    tspec = pl.BlockSpec((TM_TAB, LANES), lambda i: (i, 0))
    vspec = pl.BlockSpec((1, LANES), lambda i: (0, 0))
    return pl.pallas_call(
        functools.partial(_setup_kernel, n_cast=casts.n),
        grid=(steps,),
        in_specs=[
            pl.BlockSpec((TM_TAB, 1), lambda i: (i, 0)), vspec, vspec,
            pl.BlockSpec((B, D), lambda i: (0, 0)),
            pl.BlockSpec((L, D, N // steps), lambda i: (0, 0, i)),
            pl.BlockSpec((L, 1, N // steps), lambda i: (0, 0, i)),
            pl.BlockSpec((D, NK // steps), lambda i: (0, i)),
            pl.BlockSpec((1, NK // steps), lambda i: (0, i)),
        ] + casts.in_specs,
        out_specs=[tspec] * 4 + [
            pl.BlockSpec((L, B, N // steps), lambda i: (0, 0, i)),
            pl.BlockSpec((B, NK // steps), lambda i: (0, i)),
        ] + casts.out_specs,
        out_shape=[tab] * 4 + [jax.ShapeDtypeStruct((L, B, N), F32), jax.ShapeDtypeStruct((B, NK), F32)]
        + casts.out_shapes,
        compiler_params=_cparams("parallel"),
        name="setup",
    )(positions.reshape(T, 1), inv_r.reshape(1, LANES), inv_d_lanes.reshape(1, LANES), c, ada_w,
      ada_b.reshape(L, 1, N), kv_ada_w, kv_ada_b.reshape(1, NK), *casts.arrays)


def _rotate_half_matrix():
    n = 2 * LANES
    half = ROPE_DIM // 2
    row = lax.broadcasted_iota(jnp.int32, (n, n), 0)
    col = lax.broadcasted_iota(jnp.int32, (n, n), 1)
    cm = jnp.bitwise_and(col, DIFF_HEAD_DIM - 1)
    return jnp.where((cm < half) & (row == col + half), -1.0,
                     jnp.where((cm >= half) & (cm < ROPE_DIM) & (row == col - half), 1.0, 0.0)).astype(BF16)


def _proj_kernel(x_ref, g_ref, sc_ref, sh_ref, w_ref, c_ref, s_ref, rot_ref, o_ref, *, rope_cols, scale):
    pair = 2 * LANES
    halves = [slice(r * SUB_PROJ, (r + 1) * SUB_PROJ) for r in range(TM_PROJ // SUB_PROJ)]
    hs = [((_rms(x_ref[rows, :]) * g_ref[...]) * (1.0 + sc_ref[...]) + sh_ref[...]).astype(BF16) for rows in halves]
    units = [(r, cc) for r in range(len(halves)) for cc in range(w_ref.shape[1] // TN_PROJ)]

    def main_dot(u):
        r, cc = units[u]
        return _dot(hs[r], w_ref[:, cc * TN_PROJ:(cc + 1) * TN_PROJ])

    acc_next = main_dot(0)
    for u, (r, cc) in enumerate(units):
        acc = acc_next
        if u + 1 < len(units):
            acc_next = main_dot(u + 1)
        rows = halves[r]
        for p in range(TN_PROJ // pair):
            xc = acc[:, p * pair:(p + 1) * pair]
            roped = cc * TN_PROJ + p * pair < rope_cols
            if roped:
                partner = _dot(xc.astype(BF16), rot_ref[...])
            for v in range(2):
                lanes = slice(v * LANES, (v + 1) * LANES)
                slab = (cc * TN_PROJ + p * pair) // LANES + v
                if roped:
                    o_ref[slab, rows, :] = ((xc[:, lanes] * c_ref[rows, :] + partner[:, lanes] * s_ref[rows, :])
                                            * scale).astype(BF16)
                else:
                    o_ref[slab, rows, :] = xc[:, lanes].astype(BF16)


def _proj(x, g, sc, sh, w, c_tab, s_tab, *, rope_cols, scale, batch, seq):
    T, D = x.shape
    N = w.shape[1]
    tpb = seq // TM_PROJ
    vec = pl.BlockSpec((1, D), lambda i: (0, 0))
    bvec = pl.BlockSpec((None, 1, D), lambda i: (i // tpb, 0, 0))
    tspec = pl.BlockSpec((TM_PROJ, LANES), lambda i: (i, 0))
    kern = functools.partial(_proj_kernel, rope_cols=rope_cols, scale=scale)
    return pl.pallas_call(
        kern,
        grid=(T // TM_PROJ,),
        in_specs=[
            pl.BlockSpec((TM_PROJ, D), lambda i: (i, 0)), vec, bvec, bvec,
            _resident(w), tspec, tspec,
            pl.BlockSpec((2 * LANES, 2 * LANES), lambda i: (0, 0)),
        ],
        out_specs=pl.BlockSpec((None, N // LANES, TM_PROJ, LANES), lambda i: (i // tpb, 0, i % tpb, 0)),
        out_shape=jax.ShapeDtypeStruct((batch, N // LANES, seq, LANES), BF16),
        compiler_params=_cparams("parallel"),
        name="proj_heads",
    )(x, g, sc, sh, w, c_tab, s_tab, _rotate_half_matrix())


def _ret_kernel(x_ref, g_ref, sc_ref, sh_ref, cos_ref, sin_ref, w_ref, *rest, n_cast):
    cast_src, z_ref, cast_dst = rest[:n_cast], rest[n_cast], rest[n_cast + 1:2 * n_cast + 1]
    r_ref, dm_ref, xi_ref, ze_ref = rest[2 * n_cast + 1:]
    _Casts.run(cast_src, cast_dst)
    C = RET_CHUNK
    b, n = pl.program_id(0), pl.program_id(1)
    log_g = [math.log1p(-(2.0 ** (-5 - h))) for h in range(RET_HEADS)]

    @pl.when(n == 0)
    def _():
        r_ref[...] = jnp.zeros_like(r_ref)

    @pl.when((b == 0) & (n == 0))
    def _():
        diff = (lax.broadcasted_iota(jnp.int32, (C, C), 0) - lax.broadcasted_iota(jnp.int32, (C, C), 1)).astype(F32)
        idx = lax.broadcasted_iota(jnp.int32, (C, RET_V_DIM), 0).astype(F32)
        for h in range(RET_HEADS):
            dm_ref[h] = jnp.where(diff >= 0, jnp.exp(jnp.maximum(diff, 0.0) * log_g[h]), 0.0)
            xi_ref[h] = jnp.exp((idx + 1.0) * log_g[h])
            ze_ref[h] = jnp.exp((C - 1.0 - idx) * log_g[h])

    for ci in range(RET_CHUNKS_PER_STEP):
        _ret_chunk(slice(ci * C, (ci + 1) * C), log_g, x_ref, g_ref, sc_ref, sh_ref, cos_ref, sin_ref, w_ref, z_ref,
                   r_ref, dm_ref, xi_ref, ze_ref)


def _ret_chunk(rows, log_g, x_ref, g_ref, sc_ref, sh_ref, cos_ref, sin_ref, w_ref, z_ref, r_ref, dm_ref, xi_ref, ze_ref):
    C = RET_CHUNK
    HK = RET_HEADS * RET_QK_DIM
    HV = RET_HEADS * RET_V_DIM
    hin = ((_rms(x_ref[rows, :]) * g_ref[...]) * (1.0 + sc_ref[...]) + sh_ref[...]).astype(BF16)
    cos = cos_ref[rows, :]
    sin = sin_ref[rows, :]

    def roped(col, mul):
        t = _dot(hin, w_ref[:, col:col + RET_QK_DIM])
        x1, x2 = t[:, :LANES], t[:, LANES:]
        return jnp.concatenate([(x1 * cos - x2 * sin) * mul, (x2 * cos + x1 * sin) * mul], axis=1).astype(BF16)

    for h in range(RET_HEADS):
        qh = roped(h * RET_QK_DIM, 1.0)
        kh = roped(HK + h * RET_QK_DIM, RET_QK_DIM ** -0.5)
        vh = _dot(hin, w_ref[:, 2 * HK + h * RET_V_DIM:2 * HK + (h + 1) * RET_V_DIM])
        gt = _dot(hin, w_ref[:, 2 * HK + HV + h * RET_V_DIM:2 * HK + HV + (h + 1) * RET_V_DIM])
        s = _dot_nt(qh, kh)
        rh = r_ref[h]
        cross = _dot(qh, rh.astype(BF16)) * xi_ref[h]
        r_ref[h] = rh * math.exp(C * log_g[h]) + _dot_tn(kh, (vh * ze_ref[h]).astype(BF16))
        intra = _dot((s * dm_ref[h]).astype(BF16), vh.astype(BF16))
        o = _rms(intra + cross)
        z_ref[rows, h * RET_V_DIM:(h + 1) * RET_V_DIM] = (gt * (1.0 / (1.0 + jnp.exp(-gt))) * o).astype(BF16)


def _retention_layer(x, g, sc, sh, cos, sin, w_in, cast_sources, *, batch, seq):
    T, D = x.shape
    C = RET_CHUNK
    rows = RET_CHUNKS_PER_STEP * C
    nch = seq // rows
    HV = RET_HEADS * RET_V_DIM
    row = lambda b, n: (b * nch + n, 0)
    vec = pl.BlockSpec((1, D), lambda b, n: (0, 0))
    bvec = pl.BlockSpec((None, 1, D), lambda b, n: (b, 0, 0))
    tspec = pl.BlockSpec((rows, LANES), row)
    casts = _Casts(cast_sources, batch * nch, lambda b, n: b * nch + n)
    return pl.pallas_call(
        functools.partial(_ret_kernel, n_cast=casts.n),
        grid=(batch, nch),
        in_specs=[pl.BlockSpec((rows, D), row), vec, bvec, bvec, tspec, tspec, _resident(w_in)] + casts.in_specs,
        out_specs=[pl.BlockSpec((rows, HV), row)] + casts.out_specs,
        out_shape=[jax.ShapeDtypeStruct((T, HV), BF16)] + casts.out_shapes,
        scratch_shapes=[
            pltpu.VMEM((RET_HEADS, RET_QK_DIM, RET_V_DIM), F32),
            pltpu.VMEM((RET_HEADS, C, C), F32),
            pltpu.VMEM((RET_HEADS, C, RET_V_DIM), F32),
            pltpu.VMEM((RET_HEADS, C, RET_V_DIM), F32),
        ],
        compiler_params=_cparams("arbitrary", "arbitrary"),
        name="retention",
    )(x, g, sc, sh, cos, sin, w_in, *casts.arrays)


def _attn_kernel(lam_ref, q_ref, k_ref, v_ref, sg_ref, z_ref, vt_ref, *, lambda_init):
    S = q_ref.shape[1]
    lf = lam_ref[...]
    lam = (jnp.exp(jnp.sum(lf[0:1] * lf[1:2], axis=-1, keepdims=True))
           - jnp.exp(jnp.sum(lf[2:3] * lf[3:4], axis=-1, keepdims=True)) + lambda_init)
    lane = lax.broadcasted_iota(jnp.int32, (TQ, LANES), 1)
    causal_t = lax.broadcasted_iota(jnp.int32, (TQ, TQ), 0) <= lax.broadcasted_iota(jnp.int32, (TQ, TQ), 1)
    for pr in range(ATTN_PAIRS):
        vt_ref[pr, :DIFF_V_DIM, :] = v_ref[pr].T
        vt_ref[pr, DIFF_V_DIM:, :] = jnp.ones((vt_ref.shape[1] - DIFF_V_DIM, S), BF16)

    def scores_t(unit):
        i, pr = unit
        n_keys = (i + 1) * TQ
        qp = q_ref[pr, i * TQ:(i + 1) * TQ, :]
        zero = jnp.zeros_like(qp)
        keys = k_ref[pr, :n_keys, :]
        return (_dot_nt(keys, jnp.where(lane < DIFF_HEAD_DIM, qp, zero)),
                _dot_nt(keys, jnp.where(lane < DIFF_HEAD_DIM, zero, qp)))

    def weighted_values_t(pr, st):
        n_main = st.shape[0] - TQ
        sd = jnp.where(causal_t, st[n_main:, :], -jnp.inf)
        m = jnp.max(sd, axis=0, keepdims=True)
        if n_main:
            m = jnp.maximum(m, jnp.max(st[:n_main, :], axis=0, keepdims=True))
        r = _dot(vt_ref[pr, :, n_main:n_main + TQ], jnp.exp2(sd - m).astype(BF16))
        if n_main:
            r = r + _dot(vt_ref[pr, :, :n_main], jnp.exp2(st[:n_main, :] - m).astype(BF16))
        return r

    units = [(i, pr) for i in range(S // TQ) for pr in range(ATTN_PAIRS)]
    ahead = [scores_t(u) for u in units[:ATTN_LOOKAHEAD]]
    for pos, (i, pr) in enumerate(units):
        n_keys = (i + 1) * TQ
        st1, st2 = ahead.pop(0)
        if pos + ATTN_LOOKAHEAD < len(units):
            ahead.append(scores_t(units[pos + ATTN_LOOKAHEAD]))
        r1 = weighted_values_t(pr, st1)
        r2 = weighted_values_t(pr, st2)
        ot = (r1[:DIFF_V_DIM] * (1.0 / r1[DIFF_V_DIM:DIFF_V_DIM + 1])
              - lam * (r2[:DIFF_V_DIM] * (1.0 / r2[DIFF_V_DIM:DIFF_V_DIM + 1])))
        z_ref[i * TQ:(i + 1) * TQ, pr * LANES:(pr + 1) * LANES] = (
            (_rms(ot.T) * sg_ref[...]) * (1.0 - lambda_init)).astype(BF16)


def _diff_attention(q, kv, lam, sg, *, lambda_init):
    B, H, S, _ = q.shape
    hp = H // ATTN_PAIRS
    slab = lambda off: pl.BlockSpec((None, ATTN_PAIRS, S, LANES), lambda b, p: (b, p + off, 0, 0))
    kern = functools.partial(_attn_kernel, lambda_init=lambda_init)
    return pl.pallas_call(
        kern,
        grid=(B, hp),
        in_specs=[
            pl.BlockSpec(lam.shape, lambda b, p: (0, 0)),
            slab(0), slab(0), slab(hp),
            pl.BlockSpec((1, DIFF_V_DIM), lambda b, p: (0, 0)),
        ],
        out_specs=pl.BlockSpec((S, ATTN_PAIRS * LANES), lambda b, p: (b, p)),
        out_shape=jax.ShapeDtypeStruct((B * S, H * LANES), BF16),
        scratch_shapes=[pltpu.VMEM((ATTN_PAIRS, DIFF_V_DIM + BF16_SUBLANES, S), BF16)],
        compiler_params=_cparams("parallel", "parallel"),
        name="diff_attention",
    )(lam, q, kv, kv, sg)


def _post_mlp_kernel(z_ref, wo_ref, x_ref, gaa_ref, g1_ref, g2_ref, scm_ref, shm_ref, w1_ref, w2_ref, gam_ref, g3_ref,
                     *rest, n_cast):
    cast_src, xo_ref, cast_dst, a_ref = rest[:n_cast], rest[n_cast], rest[n_cast + 1:2 * n_cast + 1], rest[-1]
    _Casts.run(cast_src, cast_dst)
    starts = [sum(SUBS_MLP[:r]) for r in range(len(SUBS_MLP))]
    halves = [slice(s0, s0 + n) for s0, n in zip(starts, SUBS_MLP)]
    ys = [_dot(z_ref[rows, :], wo_ref[...]) for rows in halves]
    hs = []
    for rows, y in zip(halves, ys):
        xn = x_ref[rows, :] + (1.0 + gaa_ref[...]) * (_rms(y) * g1_ref[...])
        xo_ref[rows, :] = xn
        hs.append(((_rms(xn) * g2_ref[...]) * (1.0 + scm_ref[...]) + shm_ref[...]).astype(BF16))
    for rows, h in zip(halves, hs):
        for f in range(a_ref.shape[1] // TF_MLP):
            cols = slice(f * TF_MLP, (f + 1) * TF_MLP)
            a = jnp.maximum(_dot(h, w1_ref[:, cols]), 0.0)
            a_ref[rows, cols] = (a * a).astype(BF16)
    y2s = [_dot(a_ref[rows, :], w2_ref[...]) for rows in halves]
    for rows, y2 in zip(halves, y2s):
        xo_ref[rows, :] = xo_ref[rows, :] + (1.0 + gam_ref[...]) * (_rms(y2) * g3_ref[...])


def _post_mlp(z, wo, x, gaa, g1, g2, scm, shm, w1, w2, gam, g3, cast_sources, *, seq):
    T, D = x.shape
    KZ = z.shape[1]
    F = w1.shape[1]
    tpb = seq // TM_MLP
    xspec = pl.BlockSpec((TM_MLP, D), lambda i: (i, 0))
    vec = pl.BlockSpec((1, D), lambda i: (0, 0))
    bvec = pl.BlockSpec((None, 1, D), lambda i: (i // tpb, 0, 0))
    casts = _Casts(cast_sources, T // TM_MLP, lambda i: i)
    return pl.pallas_call(
        functools.partial(_post_mlp_kernel, n_cast=casts.n),
        grid=(T // TM_MLP,),
        in_specs=[
            pl.BlockSpec((TM_MLP, KZ), lambda i: (i, 0)), _resident(wo), xspec,
            bvec, vec, vec, bvec, bvec,
            _resident(w1), _resident(w2), bvec, vec,
        ] + casts.in_specs,
        out_specs=[xspec] + casts.out_specs,
        out_shape=[jax.ShapeDtypeStruct((T, D), F32)] + casts.out_shapes,
        scratch_shapes=[pltpu.VMEM((TM_MLP, F), BF16)],
        compiler_params=_cparams("parallel"),
        name="post_mlp",
    )(z, wo, x, gaa, g1, g2, scm, shm, w1, w2, gam, g3, *casts.arrays)


def kernel(x, c, positions, norm_g, ada_w, ada_b, ret_w_in, ret_w_out, kv_norm_g, kv_ada_w, kv_ada_b, kv_w,
           diff_w_q, diff_w_o, diff_lam, diff_subln_g, mlp_w1, mlp_w2):
    B, S, D = x.shape
    T = B * S
    xf = x.reshape(T, D)

    gvec = lambda l, i: norm_g[l, i].reshape(1, D)
    n_b = DEPTH - N_A

    ret_cos, ret_sin, d_cos, d_sin, mod, kv_mod, w_in = _setup(
        positions, c, ada_w, ada_b, kv_ada_w, kv_ada_b, [(ret_w_in, 0)])
    w_kv = w_q = w_o = kv = None

    def mvec(l, i):
        return mod[l, :, i * D:(i + 1) * D].reshape(B, 1, D)

    for l in range(DEPTH):
        sh_a, sc_a, ga_a, sh_m, sc_m, ga_m = (mvec(l, i) for i in range(6))
        if l < N_A:
            if l + 1 < N_A:
                mixer_next = [(ret_w_in, l + 1)]
            else:
                mixer_next = [(kv_w[None], 0)] + [(diff_w_q, j) for j in range(n_b)] + [
                    (diff_w_o, j) for j in range(n_b)]
            z, wo, w1, w2, *nxt = _retention_layer(
                xf, gvec(l, 0), sc_a, sh_a, ret_cos, ret_sin, w_in,
                [(ret_w_out, l), (mlp_w1, l), (mlp_w2, l)] + mixer_next, batch=B, seq=S)
            if l + 1 < N_A:
                (w_in,) = nxt
            else:
                w_kv, w_q, w_o = nxt[0], nxt[1:1 + n_b], nxt[1 + n_b:]
        else:
            j = l - N_A
            if j == 0:
                kv = _proj(xf, kv_norm_g.reshape(1, D), kv_mod[:, D:].reshape(B, 1, D),
                           kv_mod[:, :D].reshape(B, 1, D), w_kv, d_cos, d_sin, rope_cols=D, scale=1.0, batch=B, seq=S)
            q = _proj(xf, gvec(l, 0), sc_a, sh_a, w_q[j], d_cos, d_sin,
                      rope_cols=D, scale=DIFF_HEAD_DIM ** -0.5 * math.log2(math.e), batch=B, seq=S)
            z = _diff_attention(q, kv, diff_lam[j], diff_subln_g[j].reshape(1, DIFF_V_DIM),
                                lambda_init=0.8 - 0.6 * math.exp(-0.3 * l))
            wo = w_o[j]
        mlp_next = [(mlp_w1, l + 1), (mlp_w2, l + 1)] if N_A <= l + 1 < DEPTH else []
        xf, *nxt = _post_mlp(z, wo, xf, ga_a, gvec(l, 1), gvec(l, 2), sc_m, sh_m, w1, w2, ga_m, gvec(l, 3),
                             mlp_next, seq=S)
        if nxt:
            w1, w2 = nxt
    return xf.reshape(B, S, D)
```

```python
import functools
import math

import jax
import jax.numpy as jnp
from jax import lax
from jax.experimental import pallas as pl
from jax.experimental.pallas import tpu as pltpu

D_MODEL = 1024
DEPTH = 4
N_A = DEPTH // 2
RET_HEADS = 4
RET_QK_DIM = D_MODEL // RET_HEADS
RET_V_DIM = 2 * RET_QK_DIM
RET_ROPE_BASE = 10000.0
DIFF_HEAD_DIM = 64
DIFF_V_DIM = 2 * DIFF_HEAD_DIM
ROPE_THETA = 500000.0
ROPE_DIM = DIFF_HEAD_DIM // 4
EPS = 1e-6

LANES = 128
BF16_SUBLANES = 16
VMEM_LIMIT = 56 * 1024 * 1024

TM_PROJ = 1024
SUB_PROJ = 512
TN_PROJ = 512
TM_MLP = 512
SUBS_MLP = (256, 256)
TF_MLP = 512
RET_CHUNK = 256
RET_CHUNKS_PER_STEP = 2
TQ = 256
ATTN_PAIRS = 2
ATTN_LOOKAHEAD = 6
TM_TAB = 1024

F32 = jnp.float32
BF16 = jnp.bfloat16


def _cparams(*sem):
    return pltpu.CompilerParams(dimension_semantics=sem, vmem_limit_bytes=VMEM_LIMIT)


def _resident(w):
    return pl.BlockSpec(w.shape, lambda *_: (0, 0), pipeline_mode=pl.Buffered(1))


class _Casts:
    def __init__(self, sources, steps, step_of):
        self.arrays = [stack for stack, _ in sources]
        self.in_specs, self.out_specs, self.out_shapes = [], [], []
        for stack, layer in sources:
            _, rows, cols = stack.shape
            blk = rows // steps
            self.in_specs.append(pl.BlockSpec((None, blk, cols), lambda *g, layer=layer: (layer, step_of(*g), 0)))
            self.out_specs.append(pl.BlockSpec((blk, cols), lambda *g: (step_of(*g), 0)))
            self.out_shapes.append(jax.ShapeDtypeStruct((rows, cols), BF16))
        self.n = len(sources)

    @staticmethod
    def run(src_refs, dst_refs):
        for src, dst in zip(src_refs, dst_refs):
            dst[...] = src[...].astype(BF16)


def _rms(x):
    return x * lax.rsqrt(jnp.mean(x * x, axis=-1, keepdims=True) + EPS)


def _dot(a, b):
    return jnp.dot(a, b, preferred_element_type=F32)


def _dot_nt(a, b):
    return lax.dot_general(a, b, (((1,), (1,)), ((), ())), preferred_element_type=F32)


def _dot_tn(a, b):
    return lax.dot_general(a, b, (((0,), (0,)), ((), ())), preferred_element_type=F32)


def _setup_kernel(pos_ref, invr_ref, invd_ref, c_ref, aw_ref, ab_ref, kw_ref, kb_ref, *rest, n_cast):
    cast_src, cast_dst = rest[:n_cast], rest[n_cast + 6:]
    rc_ref, rs_ref, dc_ref, ds_ref, mod_ref, kmod_ref = rest[n_cast:n_cast + 6]
    _Casts.run(cast_src, cast_dst)

    c = c_ref[...]
    c_act = (c * (1.0 / (1.0 + jnp.exp(-c)))).astype(BF16)
    for l in range(aw_ref.shape[0]):
        mod_ref[l] = _dot(c_act, aw_ref[l].astype(BF16)) + ab_ref[l]
    kmod_ref[...] = _dot(c_act, kw_ref[...].astype(BF16)) + kb_ref[...]

    p = pos_ref[...].astype(F32)
    ang = p * invr_ref[...]
    rc_ref[...] = jnp.cos(ang)
    rs_ref[...] = jnp.sin(ang)
    angd = p * invd_ref[...]
    rotary = jnp.bitwise_and(lax.broadcasted_iota(jnp.int32, angd.shape, 1), DIFF_HEAD_DIM - 1) < ROPE_DIM
    dc_ref[...] = jnp.where(rotary, jnp.cos(angd), 1.0)
    ds_ref[...] = jnp.where(rotary, jnp.sin(angd), 0.0)


def _setup(positions, c, ada_w, ada_b, kv_ada_w, kv_ada_b, cast_sources):
    T = positions.size
    steps = T // TM_TAB
    L, D, N = ada_w.shape
    NK = kv_ada_w.shape[1]
    B = c.shape[0]
    casts = _Casts(cast_sources, steps, lambda i: i)
    inv_r = RET_ROPE_BASE ** (-jnp.arange(0, RET_QK_DIM, 2, dtype=F32) / RET_QK_DIM)
    inv_d = ROPE_THETA ** (-jnp.arange(0, ROPE_DIM, 2, dtype=F32) / ROPE_DIM)
    pat = jnp.concatenate([inv_d, inv_d, jnp.zeros((DIFF_HEAD_DIM - ROPE_DIM,), F32)])
    inv_d_lanes = jnp.tile(pat, LANES // DIFF_HEAD_DIM)
    tab = jax.ShapeDtypeStruct((T, LANES), F32)
    tspec = pl.BlockSpec((TM_TAB, LANES), lambda i: (i, 0))
    vspec = pl.BlockSpec((1, LANES), lambda i: (0, 0))
    return pl.pallas_call(
        functools.partial(_setup_kernel, n_cast=casts.n),
        grid=(steps,),
        in_specs=[
            pl.BlockSpec((TM_TAB, 1), lambda i: (i, 0)), vspec, vspec,
            pl.BlockSpec((B, D), lambda i: (0, 0)),
            pl.BlockSpec((L, D, N // steps), lambda i: (0, 0, i)),
            pl.BlockSpec((L, 1, N // steps), lambda i: (0, 0, i)),
            pl.BlockSpec((D, NK // steps), lambda i: (0, i)),
            pl.BlockSpec((1, NK // steps), lambda i: (0, i)),
        ] + casts.in_specs,
        out_specs=[tspec] * 4 + [
            pl.BlockSpec((L, B, N // steps), lambda i: (0, 0, i)),
            pl.BlockSpec((B, NK // steps), lambda i: (0, i)),
        ] + casts.out_specs,
        out_shape=[tab] * 4 + [jax.ShapeDtypeStruct((L, B, N), F32), jax.ShapeDtypeStruct((B, NK), F32)]
        + casts.out_shapes,
        compiler_params=_cparams("parallel"),
        name="setup",
    )(positions.reshape(T, 1), inv_r.reshape(1, LANES), inv_d_lanes.reshape(1, LANES), c, ada_w,
      ada_b.reshape(L, 1, N), kv_ada_w, kv_ada_b.reshape(1, NK), *casts.arrays)


def _rotate_half_matrix():
    n = 2 * LANES
    half = ROPE_DIM // 2
    row = lax.broadcasted_iota(jnp.int32, (n, n), 0)
    col = lax.broadcasted_iota(jnp.int32, (n, n), 1)
    cm = jnp.bitwise_and(col, DIFF_HEAD_DIM - 1)
    return jnp.where((cm < half) & (row == col + half), -1.0,
                     jnp.where((cm >= half) & (cm < ROPE_DIM) & (row == col - half), 1.0, 0.0)).astype(BF16)


def _proj_kernel(x_ref, g_ref, sc_ref, sh_ref, w_ref, c_ref, s_ref, rot_ref, o_ref, *, rope_cols, scale):
    pair = 2 * LANES
    halves = [slice(r * SUB_PROJ, (r + 1) * SUB_PROJ) for r in range(TM_PROJ // SUB_PROJ)]
    hs = [((_rms(x_ref[rows, :]) * g_ref[...]) * (1.0 + sc_ref[...]) + sh_ref[...]).astype(BF16) for rows in halves]
    units = [(r, cc) for r in range(len(halves)) for cc in range(w_ref.shape[1] // TN_PROJ)]

    def main_dot(u):
        r, cc = units[u]
        return _dot(hs[r], w_ref[:, cc * TN_PROJ:(cc + 1) * TN_PROJ])

    acc_next = main_dot(0)
    for u, (r, cc) in enumerate(units):
        acc = acc_next
        if u + 1 < len(units):
            acc_next = main_dot(u + 1)
        rows = halves[r]
        for p in range(TN_PROJ // pair):
            xc = acc[:, p * pair:(p + 1) * pair]
            roped = cc * TN_PROJ + p * pair < rope_cols
            if roped:
                partner = _dot(xc.astype(BF16), rot_ref[...])
            for v in range(2):
                lanes = slice(v * LANES, (v + 1) * LANES)
                slab = (cc * TN_PROJ + p * pair) // LANES + v
                if roped:
                    o_ref[slab, rows, :] = ((xc[:, lanes] * c_ref[rows, :] + partner[:, lanes] * s_ref[rows, :])
                                            * scale).astype(BF16)
                else:
                    o_ref[slab, rows, :] = xc[:, lanes].astype(BF16)


def _proj(x, g, sc, sh, w, c_tab, s_tab, *, rope_cols, scale, batch, seq):
    T, D = x.shape
    N = w.shape[1]
    tpb = seq // TM_PROJ
    vec = pl.BlockSpec((1, D), lambda i: (0, 0))
    bvec = pl.BlockSpec((None, 1, D), lambda i: (i // tpb, 0, 0))
    tspec = pl.BlockSpec((TM_PROJ, LANES), lambda i: (i, 0))
    kern = functools.partial(_proj_kernel, rope_cols=rope_cols, scale=scale)
    return pl.pallas_call(
        kern,
        grid=(T // TM_PROJ,),
        in_specs=[
            pl.BlockSpec((TM_PROJ, D), lambda i: (i, 0)), vec, bvec, bvec,
            _resident(w), tspec, tspec,
            pl.BlockSpec((2 * LANES, 2 * LANES), lambda i: (0, 0)),
        ],
        out_specs=pl.BlockSpec((None, N // LANES, TM_PROJ, LANES), lambda i: (i // tpb, 0, i % tpb, 0)),
        out_shape=jax.ShapeDtypeStruct((batch, N // LANES, seq, LANES), BF16),
        compiler_params=_cparams("parallel"),
        name="proj_heads",
    )(x, g, sc, sh, w, c_tab, s_tab, _rotate_half_matrix())


def _ret_kernel(x_ref, g_ref, sc_ref, sh_ref, cos_ref, sin_ref, w_ref, *rest, n_cast):
    cast_src, z_ref, cast_dst = rest[:n_cast], rest[n_cast], rest[n_cast + 1:2 * n_cast + 1]
    r_ref, dm_ref, xi_ref, ze_ref = rest[2 * n_cast + 1:]
    _Casts.run(cast_src, cast_dst)
    C = RET_CHUNK
    b, n = pl.program_id(0), pl.program_id(1)
    log_g = [math.log1p(-(2.0 ** (-5 - h))) for h in range(RET_HEADS)]

    @pl.when(n == 0)
    def _():
        r_ref[...] = jnp.zeros_like(r_ref)

    @pl.when((b == 0) & (n == 0))
    def _():
        diff = (lax.broadcasted_iota(jnp.int32, (C, C), 0) - lax.broadcasted_iota(jnp.int32, (C, C), 1)).astype(F32)
        idx = lax.broadcasted_iota(jnp.int32, (C, RET_V_DIM), 0).astype(F32)
        for h in range(RET_HEADS):
            dm_ref[h] = jnp.where(diff >= 0, jnp.exp(jnp.maximum(diff, 0.0) * log_g[h]), 0.0)
            xi_ref[h] = jnp.exp((idx + 1.0) * log_g[h])
            ze_ref[h] = jnp.exp((C - 1.0 - idx) * log_g[h])

    for ci in range(RET_CHUNKS_PER_STEP):
        _ret_chunk(slice(ci * C, (ci + 1) * C), log_g, x_ref, g_ref, sc_ref, sh_ref, cos_ref, sin_ref, w_ref, z_ref,
                   r_ref, dm_ref, xi_ref, ze_ref)


def _ret_chunk(rows, log_g, x_ref, g_ref, sc_ref, sh_ref, cos_ref, sin_ref, w_ref, z_ref, r_ref, dm_ref, xi_ref, ze_ref):
    C = RET_CHUNK
    HK = RET_HEADS * RET_QK_DIM
    HV = RET_HEADS * RET_V_DIM
    hin = ((_rms(x_ref[rows, :]) * g_ref[...]) * (1.0 + sc_ref[...]) + sh_ref[...]).astype(BF16)
    cos = cos_ref[rows, :]
    sin = sin_ref[rows, :]

    def roped(col, mul):
        t = _dot(hin, w_ref[:, col:col + RET_QK_DIM])
        x1, x2 = t[:, :LANES], t[:, LANES:]
        return jnp.concatenate([(x1 * cos - x2 * sin) * mul, (x2 * cos + x1 * sin) * mul], axis=1).astype(BF16)

    for h in range(RET_HEADS):
        qh = roped(h * RET_QK_DIM, 1.0)
        kh = roped(HK + h * RET_QK_DIM, RET_QK_DIM ** -0.5)
        vh = _dot(hin, w_ref[:, 2 * HK + h * RET_V_DIM:2 * HK + (h + 1) * RET_V_DIM])
        gt = _dot(hin, w_ref[:, 2 * HK + HV + h * RET_V_DIM:2 * HK + HV + (h + 1) * RET_V_DIM])
        s = _dot_nt(qh, kh)
        rh = r_ref[h]
        cross = _dot(qh, rh.astype(BF16)) * xi_ref[h]
        r_ref[h] = rh * math.exp(C * log_g[h]) + _dot_tn(kh, (vh * ze_ref[h]).astype(BF16))
        intra = _dot((s * dm_ref[h]).astype(BF16), vh.astype(BF16))
        o = _rms(intra + cross)
        z_ref[rows, h * RET_V_DIM:(h + 1) * RET_V_DIM] = (gt * (1.0 / (1.0 + jnp.exp(-gt))) * o).astype(BF16)


def _retention_layer(x, g, sc, sh, cos, sin, w_in, cast_sources, *, batch, seq):
    T, D = x.shape
    C = RET_CHUNK
    rows = RET_CHUNKS_PER_STEP * C
    nch = seq // rows
    HV = RET_HEADS * RET_V_DIM
    row = lambda b, n: (b * nch + n, 0)
    vec = pl.BlockSpec((1, D), lambda b, n: (0, 0))
    bvec = pl.BlockSpec((None, 1, D), lambda b, n: (b, 0, 0))
    tspec = pl.BlockSpec((rows, LANES), row)
    casts = _Casts(cast_sources, batch * nch, lambda b, n: b * nch + n)
    return pl.pallas_call(
        functools.partial(_ret_kernel, n_cast=casts.n),
        grid=(batch, nch),
        in_specs=[pl.BlockSpec((rows, D), row), vec, bvec, bvec, tspec, tspec, _resident(w_in)] + casts.in_specs,
        out_specs=[pl.BlockSpec((rows, HV), row)] + casts.out_specs,
        out_shape=[jax.ShapeDtypeStruct((T, HV), BF16)] + casts.out_shapes,
        scratch_shapes=[
            pltpu.VMEM((RET_HEADS, RET_QK_DIM, RET_V_DIM), F32),
            pltpu.VMEM((RET_HEADS, C, C), F32),
            pltpu.VMEM((RET_HEADS, C, RET_V_DIM), F32),
            pltpu.VMEM((RET_HEADS, C, RET_V_DIM), F32),
        ],
        compiler_params=_cparams("arbitrary", "arbitrary"),
        name="retention",
    )(x, g, sc, sh, cos, sin, w_in, *casts.arrays)


def _attn_kernel(lam_ref, q_ref, k_ref, v_ref, sg_ref, z_ref, vt_ref, *, lambda_init):
    S = q_ref.shape[1]
    lf = lam_ref[...]
    lam = (jnp.exp(jnp.sum(lf[0:1] * lf[1:2], axis=-1, keepdims=True))
           - jnp.exp(jnp.sum(lf[2:3] * lf[3:4], axis=-1, keepdims=True)) + lambda_init)
    lane = lax.broadcasted_iota(jnp.int32, (TQ, LANES), 1)
    causal_t = lax.broadcasted_iota(jnp.int32, (TQ, TQ), 0) <= lax.broadcasted_iota(jnp.int32, (TQ, TQ), 1)
    for pr in range(ATTN_PAIRS):
        vt_ref[pr, :DIFF_V_DIM, :] = v_ref[pr].T
        vt_ref[pr, DIFF_V_DIM:, :] = jnp.ones((vt_ref.shape[1] - DIFF_V_DIM, S), BF16)

    def scores_t(unit):
        i, pr = unit
        n_keys = (i + 1) * TQ
        qp = q_ref[pr, i * TQ:(i + 1) * TQ, :]
        zero = jnp.zeros_like(qp)
        keys = k_ref[pr, :n_keys, :]
        return (_dot_nt(keys, jnp.where(lane < DIFF_HEAD_DIM, qp, zero)),
                _dot_nt(keys, jnp.where(lane < DIFF_HEAD_DIM, zero, qp)))

    def weighted_values_t(pr, st):
        n_main = st.shape[0] - TQ
        sd = jnp.where(causal_t, st[n_main:, :], -jnp.inf)
        m = jnp.max(sd, axis=0, keepdims=True)
        if n_main:
            m = jnp.maximum(m, jnp.max(st[:n_main, :], axis=0, keepdims=True))
        r = _dot(vt_ref[pr, :, n_main:n_main + TQ], jnp.exp2(sd - m).astype(BF16))
        if n_main:
            r = r + _dot(vt_ref[pr, :, :n_main], jnp.exp2(st[:n_main, :] - m).astype(BF16))
        return r

    units = [(i, pr) for i in range(S // TQ) for pr in range(ATTN_PAIRS)]
    ahead = [scores_t(u) for u in units[:ATTN_LOOKAHEAD]]
    for pos, (i, pr) in enumerate(units):
        st1, st2 = ahead.pop(0)
        if pos + ATTN_LOOKAHEAD < len(units):
            ahead.append(scores_t(units[pos + ATTN_LOOKAHEAD]))
        r1 = weighted_values_t(pr, st1)
        r2 = weighted_values_t(pr, st2)
        ot = (r1[:DIFF_V_DIM] * (1.0 / r1[DIFF_V_DIM:DIFF_V_DIM + 1])
              - lam * (r2[:DIFF_V_DIM] * (1.0 / r2[DIFF_V_DIM:DIFF_V_DIM + 1])))
        z_ref[i * TQ:(i + 1) * TQ, pr * LANES:(pr + 1) * LANES] = (
            (_rms(ot.T) * sg_ref[...]) * (1.0 - lambda_init)).astype(BF16)


def _diff_attention(q, kv, lam, sg, *, lambda_init):
    B, H, S, _ = q.shape
    hp = H // ATTN_PAIRS
    slab = lambda off: pl.BlockSpec((None, ATTN_PAIRS, S, LANES), lambda b, p: (b, p + off, 0, 0))
    kern = functools.partial(_attn_kernel, lambda_init=lambda_init)
    return pl.pallas_call(
        kern,
        grid=(B, hp),
        in_specs=[
            pl.BlockSpec(lam.shape, lambda b, p: (0, 0)),
            slab(0), slab(0), slab(hp),
            pl.BlockSpec((1, DIFF_V_DIM), lambda b, p: (0, 0)),
        ],
        out_specs=pl.BlockSpec((S, ATTN_PAIRS * LANES), lambda b, p: (b, p)),
        out_shape=jax.ShapeDtypeStruct((B * S, H * LANES), BF16),
        scratch_shapes=[pltpu.VMEM((ATTN_PAIRS, DIFF_V_DIM + BF16_SUBLANES, S), BF16)],
        compiler_params=_cparams("parallel", "parallel"),
        name="diff_attention",
    )(lam, q, kv, kv, sg)


def _post_mlp_kernel(z_ref, wo_ref, x_ref, gaa_ref, g1_ref, g2_ref, scm_ref, shm_ref, w1_ref, w2_ref, gam_ref, g3_ref,
                     *rest, n_cast):
    cast_src, xo_ref, cast_dst, a_ref = rest[:n_cast], rest[n_cast], rest[n_cast + 1:2 * n_cast + 1], rest[-1]
    _Casts.run(cast_src, cast_dst)
    starts = [sum(SUBS_MLP[:r]) for r in range(len(SUBS_MLP))]
    halves = [slice(s0, s0 + n) for s0, n in zip(starts, SUBS_MLP)]
    ys = [_dot(z_ref[rows, :], wo_ref[...]) for rows in halves]
    hs = []
    for rows, y in zip(halves, ys):
        xn = x_ref[rows, :] + (1.0 + gaa_ref[...]) * (_rms(y) * g1_ref[...])
        xo_ref[rows, :] = xn
        hs.append(((_rms(xn) * g2_ref[...]) * (1.0 + scm_ref[...]) + shm_ref[...]).astype(BF16))
    for rows, h in zip(halves, hs):
        for f in range(a_ref.shape[1] // TF_MLP):
            cols = slice(f * TF_MLP, (f + 1) * TF_MLP)
            a = jnp.maximum(_dot(h, w1_ref[:, cols]), 0.0)
            a_ref[rows, cols] = (a * a).astype(BF16)
    y2s = [_dot(a_ref[rows, :], w2_ref[...]) for rows in halves]
    for rows, y2 in zip(halves, y2s):
        xo_ref[rows, :] = xo_ref[rows, :] + (1.0 + gam_ref[...]) * (_rms(y2) * g3_ref[...])


def _post_mlp(z, wo, x, gaa, g1, g2, scm, shm, w1, w2, gam, g3, cast_sources, *, seq):
    T, D = x.shape
    KZ = z.shape[1]
    F = w1.shape[1]
    tpb = seq // TM_MLP
    xspec = pl.BlockSpec((TM_MLP, D), lambda i: (i, 0))
    vec = pl.BlockSpec((1, D), lambda i: (0, 0))
    bvec = pl.BlockSpec((None, 1, D), lambda i: (i // tpb, 0, 0))
    casts = _Casts(cast_sources, T // TM_MLP, lambda i: i)
    return pl.pallas_call(
        functools.partial(_post_mlp_kernel, n_cast=casts.n),
        grid=(T // TM_MLP,),
        in_specs=[
            pl.BlockSpec((TM_MLP, KZ), lambda i: (i, 0)), _resident(wo), xspec,
            bvec, vec, vec, bvec, bvec,
            _resident(w1), _resident(w2), bvec, vec,
        ] + casts.in_specs,
        out_specs=[xspec] + casts.out_specs,
        out_shape=[jax.ShapeDtypeStruct((T, D), F32)] + casts.out_shapes,
        scratch_shapes=[pltpu.VMEM((TM_MLP, F), BF16)],
        compiler_params=_cparams("parallel"),
        name="post_mlp",
    )(z, wo, x, gaa, g1, g2, scm, shm, w1, w2, gam, g3, *casts.arrays)


def kernel(x, c, positions, norm_g, ada_w, ada_b, ret_w_in, ret_w_out, kv_norm_g, kv_ada_w, kv_ada_b, kv_w,
           diff_w_q, diff_w_o, diff_lam, diff_subln_g, mlp_w1, mlp_w2):
    B, S, D = x.shape
    T = B * S
    assert D == D_MODEL and norm_g.shape[0] == DEPTH
    assert S % TM_PROJ == 0 and S % TM_MLP == 0 and S % TQ == 0 and S % (RET_CHUNKS_PER_STEP * RET_CHUNK) == 0
    assert T % TM_TAB == 0 and sum(SUBS_MLP) == TM_MLP and TM_PROJ % SUB_PROJ == 0
    xf = x.reshape(T, D)

    gvec = lambda l, i: norm_g[l, i].reshape(1, D)
    n_b = DEPTH - N_A

    ret_cos, ret_sin, d_cos, d_sin, mod, kv_mod, w_in = _setup(
        positions, c, ada_w, ada_b, kv_ada_w, kv_ada_b, [(ret_w_in, 0)])
    w_kv = w_q = w_o = kv = None

    def mvec(l, i):
        return mod[l, :, i * D:(i + 1) * D].reshape(B, 1, D)

    for l in range(DEPTH):
        sh_a, sc_a, ga_a, sh_m, sc_m, ga_m = (mvec(l, i) for i in range(6))
        if l < N_A:
            if l + 1 < N_A:
                mixer_next = [(ret_w_in, l + 1)]
            else:
                mixer_next = [(kv_w[None], 0)] + [(diff_w_q, j) for j in range(n_b)] + [
                    (diff_w_o, j) for j in range(n_b)]
            z, wo, w1, w2, *nxt = _retention_layer(
                xf, gvec(l, 0), sc_a, sh_a, ret_cos, ret_sin, w_in,
                [(ret_w_out, l), (mlp_w1, l), (mlp_w2, l)] + mixer_next, batch=B, seq=S)
            if l + 1 < N_A:
                (w_in,) = nxt
            else:
                w_kv, w_q, w_o = nxt[0], nxt[1:1 + n_b], nxt[1 + n_b:]
        else:
            j = l - N_A
            if j == 0:
                kv = _proj(xf, kv_norm_g.reshape(1, D), kv_mod[:, D:].reshape(B, 1, D),
                           kv_mod[:, :D].reshape(B, 1, D), w_kv, d_cos, d_sin, rope_cols=D, scale=1.0, batch=B, seq=S)
            q = _proj(xf, gvec(l, 0), sc_a, sh_a, w_q[j], d_cos, d_sin,
                      rope_cols=D, scale=DIFF_HEAD_DIM ** -0.5 * math.log2(math.e), batch=B, seq=S)
            z = _diff_attention(q, kv, diff_lam[j], diff_subln_g[j].reshape(1, DIFF_V_DIM),
                                lambda_init=0.8 - 0.6 * math.exp(-0.3 * l))
            wo = w_o[j]
        mlp_next = [(mlp_w1, l + 1), (mlp_w2, l + 1)] if N_A <= l + 1 < DEPTH else []
        xf, *nxt = _post_mlp(z, wo, xf, ga_a, gvec(l, 1), gvec(l, 2), sc_m, sh_m, w1, w2, ga_m, gvec(l, 3),
                             mlp_next, seq=S)
        if nxt:
            w1, w2 = nxt
    return xf.reshape(B, S, D)
```

```python
import functools
import math

import jax
import jax.numpy as jnp
from jax import lax
from jax.experimental import pallas as pl
from jax.experimental.pallas import tpu as pltpu

D_MODEL = 1024
DEPTH = 4
N_A = DEPTH // 2
RET_HEADS = 4
RET_QK_DIM = D_MODEL // RET_HEADS
RET_V_DIM = 2 * RET_QK_DIM
RET_ROPE_BASE = 10000.0
DIFF_HEAD_DIM = 64
DIFF_V_DIM = 2 * DIFF_HEAD_DIM
ROPE_THETA = 500000.0
ROPE_DIM = DIFF_HEAD_DIM // 4
EPS = 1e-6

LANES = 128
BF16_SUBLANES = 16
VMEM_LIMIT = 56 * 1024 * 1024

TM_PROJ = 1024
SUBS_PROJ = (256, 256, 512)
TN_PROJ = 512
TM_MLP = 512
SUBS_MLP = (256, 256)
TF_MLP = 512
RET_CHUNK = 256
RET_CHUNKS_PER_STEP = 2
TQ = 256
ATTN_PAIRS = 2
ATTN_LOOKAHEAD = 6
TM_TAB = 1024

F32 = jnp.float32
BF16 = jnp.bfloat16


def _cparams(*sem):
    return pltpu.CompilerParams(dimension_semantics=sem, vmem_limit_bytes=VMEM_LIMIT)


def _resident(w):
    return pl.BlockSpec(w.shape, lambda *_: (0, 0), pipeline_mode=pl.Buffered(1))


class _Casts:
    def __init__(self, sources, steps, step_of):
        self.arrays = [stack for stack, _ in sources]
        self.in_specs, self.out_specs, self.out_shapes = [], [], []
        for stack, layer in sources:
            _, rows, cols = stack.shape
            blk = rows // steps
            self.in_specs.append(pl.BlockSpec((None, blk, cols), lambda *g, layer=layer: (layer, step_of(*g), 0)))
            self.out_specs.append(pl.BlockSpec((blk, cols), lambda *g: (step_of(*g), 0)))
            self.out_shapes.append(jax.ShapeDtypeStruct((rows, cols), BF16))
        self.n = len(sources)

    @staticmethod
    def run(src_refs, dst_refs):
        for src, dst in zip(src_refs, dst_refs):
            dst[...] = src[...].astype(BF16)


def _rms(x):
    return x * lax.rsqrt(jnp.mean(x * x, axis=-1, keepdims=True) + EPS)


def _dot(a, b):
    return jnp.dot(a, b, preferred_element_type=F32)


def _dot_nt(a, b):
    return lax.dot_general(a, b, (((1,), (1,)), ((), ())), preferred_element_type=F32)


def _dot_tn(a, b):
    return lax.dot_general(a, b, (((0,), (0,)), ((), ())), preferred_element_type=F32)


def _setup_kernel(pos_ref, invr_ref, invd_ref, c_ref, aw_ref, ab_ref, kw_ref, kb_ref, *rest, n_cast):
    cast_src, cast_dst = rest[:n_cast], rest[n_cast + 6:]
    rc_ref, rs_ref, dc_ref, ds_ref, mod_ref, kmod_ref = rest[n_cast:n_cast + 6]
    _Casts.run(cast_src, cast_dst)

    c = c_ref[...]
    c_act = (c * (1.0 / (1.0 + jnp.exp(-c)))).astype(BF16)
    for l in range(aw_ref.shape[0]):
        mod_ref[l] = _dot(c_act, aw_ref[l].astype(BF16)) + ab_ref[l]
    kmod_ref[...] = _dot(c_act, kw_ref[...].astype(BF16)) + kb_ref[...]

    p = pos_ref[...].astype(F32)
    ang = p * invr_ref[...]
    rc_ref[...] = jnp.cos(ang)
    rs_ref[...] = jnp.sin(ang)
    angd = p * invd_ref[...]
    rotary = jnp.bitwise_and(lax.broadcasted_iota(jnp.int32, angd.shape, 1), DIFF_HEAD_DIM - 1) < ROPE_DIM
    dc_ref[...] = jnp.where(rotary, jnp.cos(angd), 1.0)
    ds_ref[...] = jnp.where(rotary, jnp.sin(angd), 0.0)


def _setup(positions, c, ada_w, ada_b, kv_ada_w, kv_ada_b, cast_sources):
    T = positions.size
    steps = T // TM_TAB
    L, D, N = ada_w.shape
    NK = kv_ada_w.shape[1]
    B = c.shape[0]
    casts = _Casts(cast_sources, steps, lambda i: i)
    inv_r = RET_ROPE_BASE ** (-jnp.arange(0, RET_QK_DIM, 2, dtype=F32) / RET_QK_DIM)
    inv_d = ROPE_THETA ** (-jnp.arange(0, ROPE_DIM, 2, dtype=F32) / ROPE_DIM)
    pat = jnp.concatenate([inv_d, inv_d, jnp.zeros((DIFF_HEAD_DIM - ROPE_DIM,), F32)])
    inv_d_lanes = jnp.tile(pat, LANES // DIFF_HEAD_DIM)
    tab = jax.ShapeDtypeStruct((T, LANES), F32)
    tspec = pl.BlockSpec((TM_TAB, LANES), lambda i: (i, 0))
    vspec = pl.BlockSpec((1, LANES), lambda i: (0, 0))
    return pl.pallas_call(
        functools.partial(_setup_kernel, n_cast=casts.n),
        grid=(steps,),
        in_specs=[
            pl.BlockSpec((TM_TAB, 1), lambda i: (i, 0)), vspec, vspec,
            pl.BlockSpec((B, D), lambda i: (0, 0)),
            pl.BlockSpec((L, D, N // steps), lambda i: (0, 0, i)),
            pl.BlockSpec((L, 1, N // steps), lambda i: (0, 0, i)),
            pl.BlockSpec((D, NK // steps), lambda i: (0, i)),
            pl.BlockSpec((1, NK // steps), lambda i: (0, i)),
        ] + casts.in_specs,
        out_specs=[tspec] * 4 + [
            pl.BlockSpec((L, B, N // steps), lambda i: (0, 0, i)),
            pl.BlockSpec((B, NK // steps), lambda i: (0, i)),
        ] + casts.out_specs,
        out_shape=[tab] * 4 + [jax.ShapeDtypeStruct((L, B, N), F32), jax.ShapeDtypeStruct((B, NK), F32)]
        + casts.out_shapes,
        compiler_params=_cparams("parallel"),
        name="setup",
    )(positions.reshape(T, 1), inv_r.reshape(1, LANES), inv_d_lanes.reshape(1, LANES), c, ada_w,
      ada_b.reshape(L, 1, N), kv_ada_w, kv_ada_b.reshape(1, NK), *casts.arrays)


def _rotate_half_matrix():
    n = 2 * LANES
    half = ROPE_DIM // 2
    row = lax.broadcasted_iota(jnp.int32, (n, n), 0)
    col = lax.broadcasted_iota(jnp.int32, (n, n), 1)
    cm = jnp.bitwise_and(col, DIFF_HEAD_DIM - 1)
    return jnp.where((cm < half) & (row == col + half), -1.0,
                     jnp.where((cm >= half) & (cm < ROPE_DIM) & (row == col - half), 1.0, 0.0)).astype(BF16)


def _proj_kernel(x_ref, g_ref, sc_ref, sh_ref, w_ref, c_ref, s_ref, rot_ref, o_ref, *, rope_cols, scale):
    pair = 2 * LANES
    halves = [slice(sum(SUBS_PROJ[:r]), sum(SUBS_PROJ[:r + 1])) for r in range(len(SUBS_PROJ))]
    hs = [((_rms(x_ref[rows, :]) * g_ref[...]) * (1.0 + sc_ref[...]) + sh_ref[...]).astype(BF16) for rows in halves]
    units = [(r, cc) for r in range(len(halves)) for cc in range(w_ref.shape[1] // TN_PROJ)]

    def main_dot(u):
        r, cc = units[u]
        return _dot(hs[r], w_ref[:, cc * TN_PROJ:(cc + 1) * TN_PROJ])

    acc_next = main_dot(0)
    for u, (r, cc) in enumerate(units):
        acc = acc_next
        if u + 1 < len(units):
            acc_next = main_dot(u + 1)
        rows = halves[r]
        for p in range(TN_PROJ // pair):
            xc = acc[:, p * pair:(p + 1) * pair]
            roped = cc * TN_PROJ + p * pair < rope_cols
            if roped:
                partner = _dot(xc.astype(BF16), rot_ref[...])
            for v in range(2):
                lanes = slice(v * LANES, (v + 1) * LANES)
                slab = (cc * TN_PROJ + p * pair) // LANES + v
                if roped:
                    o_ref[slab, rows, :] = ((xc[:, lanes] * c_ref[rows, :] + partner[:, lanes] * s_ref[rows, :])
                                            * scale).astype(BF16)
                else:
                    o_ref[slab, rows, :] = xc[:, lanes].astype(BF16)


def _proj(x, g, sc, sh, w, c_tab, s_tab, *, rope_cols, scale, batch, seq):
    T, D = x.shape
    N = w.shape[1]
    tpb = seq // TM_PROJ
    vec = pl.BlockSpec((1, D), lambda i: (0, 0))
    bvec = pl.BlockSpec((None, 1, D), lambda i: (i // tpb, 0, 0))
    tspec = pl.BlockSpec((TM_PROJ, LANES), lambda i: (i, 0))
    kern = functools.partial(_proj_kernel, rope_cols=rope_cols, scale=scale)
    return pl.pallas_call(
        kern,
        grid=(T // TM_PROJ,),
        in_specs=[
            pl.BlockSpec((TM_PROJ, D), lambda i: (i, 0)), vec, bvec, bvec,
            _resident(w), tspec, tspec,
            pl.BlockSpec((2 * LANES, 2 * LANES), lambda i: (0, 0)),
        ],
        out_specs=pl.BlockSpec((None, N // LANES, TM_PROJ, LANES), lambda i: (i // tpb, 0, i % tpb, 0)),
        out_shape=jax.ShapeDtypeStruct((batch, N // LANES, seq, LANES), BF16),
        compiler_params=_cparams("parallel"),
        name="proj_heads",
    )(x, g, sc, sh, w, c_tab, s_tab, _rotate_half_matrix())


def _ret_kernel(x_ref, g_ref, sc_ref, sh_ref, cos_ref, sin_ref, w_ref, *rest, n_cast):
    cast_src, z_ref, cast_dst = rest[:n_cast], rest[n_cast], rest[n_cast + 1:2 * n_cast + 1]
    r_ref, dm_ref, xi_ref, ze_ref = rest[2 * n_cast + 1:]
    _Casts.run(cast_src, cast_dst)
    C = RET_CHUNK
    b, n = pl.program_id(0), pl.program_id(1)
    log_g = [math.log1p(-(2.0 ** (-5 - h))) for h in range(RET_HEADS)]

    @pl.when(n == 0)
    def _():
        r_ref[...] = jnp.zeros_like(r_ref)

    @pl.when((b == 0) & (n == 0))
    def _():
        diff = (lax.broadcasted_iota(jnp.int32, (C, C), 0) - lax.broadcasted_iota(jnp.int32, (C, C), 1)).astype(F32)
        idx = lax.broadcasted_iota(jnp.int32, (C, RET_V_DIM), 0).astype(F32)
        for h in range(RET_HEADS):
            dm_ref[h] = jnp.where(diff >= 0, jnp.exp(jnp.maximum(diff, 0.0) * log_g[h]), 0.0)
            xi_ref[h] = jnp.exp((idx + 1.0) * log_g[h])
            ze_ref[h] = jnp.exp((C - 1.0 - idx) * log_g[h])

    for ci in range(RET_CHUNKS_PER_STEP):
        _ret_chunk(slice(ci * C, (ci + 1) * C), log_g, x_ref, g_ref, sc_ref, sh_ref, cos_ref, sin_ref, w_ref, z_ref,
                   r_ref, dm_ref, xi_ref, ze_ref)


def _ret_chunk(rows, log_g, x_ref, g_ref, sc_ref, sh_ref, cos_ref, sin_ref, w_ref, z_ref, r_ref, dm_ref, xi_ref, ze_ref):
    C = RET_CHUNK
    HK = RET_HEADS * RET_QK_DIM
    HV = RET_HEADS * RET_V_DIM
    hin = ((_rms(x_ref[rows, :]) * g_ref[...]) * (1.0 + sc_ref[...]) + sh_ref[...]).astype(BF16)
    cos = cos_ref[rows, :]
    sin = sin_ref[rows, :]

    def roped(col, mul):
        t = _dot(hin, w_ref[:, col:col + RET_QK_DIM])
        x1, x2 = t[:, :LANES], t[:, LANES:]
        return jnp.concatenate([(x1 * cos - x2 * sin) * mul, (x2 * cos + x1 * sin) * mul], axis=1).astype(BF16)

    for h in range(RET_HEADS):
        qh = roped(h * RET_QK_DIM, 1.0)
        kh = roped(HK + h * RET_QK_DIM, RET_QK_DIM ** -0.5)
        vh = _dot(hin, w_ref[:, 2 * HK + h * RET_V_DIM:2 * HK + (h + 1) * RET_V_DIM])
        gt = _dot(hin, w_ref[:, 2 * HK + HV + h * RET_V_DIM:2 * HK + HV + (h + 1) * RET_V_DIM])
        kt = kh.T
        s = _dot(qh, kt)
        rh = r_ref[h]
        cross = _dot(qh, rh.astype(BF16)) * xi_ref[h]
        r_ref[h] = rh * math.exp(C * log_g[h]) + _dot(kt, (vh * ze_ref[h]).astype(BF16))
        intra = _dot((s * dm_ref[h]).astype(BF16), vh.astype(BF16))
        o = _rms(intra + cross)
        z_ref[rows, h * RET_V_DIM:(h + 1) * RET_V_DIM] = (gt * (1.0 / (1.0 + jnp.exp(-gt))) * o).astype(BF16)


def _retention_layer(x, g, sc, sh, cos, sin, w_in, cast_sources, *, batch, seq):
    T, D = x.shape
    C = RET_CHUNK
    rows = RET_CHUNKS_PER_STEP * C
    nch = seq // rows
    HV = RET_HEADS * RET_V_DIM
    row = lambda b, n: (b * nch + n, 0)
    vec = pl.BlockSpec((1, D), lambda b, n: (0, 0))
    bvec = pl.BlockSpec((None, 1, D), lambda b, n: (b, 0, 0))
    tspec = pl.BlockSpec((rows, LANES), row)
    casts = _Casts(cast_sources, batch * nch, lambda b, n: b * nch + n)
    return pl.pallas_call(
        functools.partial(_ret_kernel, n_cast=casts.n),
        grid=(batch, nch),
        in_specs=[pl.BlockSpec((rows, D), row), vec, bvec, bvec, tspec, tspec, _resident(w_in)] + casts.in_specs,
        out_specs=[pl.BlockSpec((rows, HV), row)] + casts.out_specs,
        out_shape=[jax.ShapeDtypeStruct((T, HV), BF16)] + casts.out_shapes,
        scratch_shapes=[
            pltpu.VMEM((RET_HEADS, RET_QK_DIM, RET_V_DIM), F32),
            pltpu.VMEM((RET_HEADS, C, C), F32),
            pltpu.VMEM((RET_HEADS, C, RET_V_DIM), F32),
            pltpu.VMEM((RET_HEADS, C, RET_V_DIM), F32),
        ],
        compiler_params=_cparams("arbitrary", "arbitrary"),
        name="retention",
    )(x, g, sc, sh, cos, sin, w_in, *casts.arrays)


def _attn_kernel(lam_ref, q_ref, k_ref, v_ref, sg_ref, z_ref, vt_ref, qt_ref, *, lambda_init):
    S = q_ref.shape[1]
    lf = lam_ref[...]
    lam = (jnp.exp(jnp.sum(lf[0:1] * lf[1:2], axis=-1, keepdims=True))
           - jnp.exp(jnp.sum(lf[2:3] * lf[3:4], axis=-1, keepdims=True)) + lambda_init)
    first_head = lax.broadcasted_iota(jnp.int32, (2 * DIFF_HEAD_DIM, TQ), 0) < DIFF_HEAD_DIM
    causal_t = lax.broadcasted_iota(jnp.int32, (TQ, TQ), 0) <= lax.broadcasted_iota(jnp.int32, (TQ, TQ), 1)
    for pr in range(ATTN_PAIRS):
        vt_ref[pr, :DIFF_V_DIM, :] = v_ref[pr].T
        vt_ref[pr, DIFF_V_DIM:, :] = jnp.ones((vt_ref.shape[1] - DIFF_V_DIM, S), BF16)
        qt_ref[pr] = q_ref[pr].T

    def scores_t(unit):
        i, pr = unit
        n_keys = (i + 1) * TQ
        qt = qt_ref[pr, :, i * TQ:(i + 1) * TQ]
        zero = jnp.zeros_like(qt)
        keys = k_ref[pr, :n_keys, :]
        return (_dot(keys, jnp.where(first_head, qt, zero)),
                _dot(keys, jnp.where(first_head, zero, qt)))

    def weighted_values_t(pr, st):
        n_main = st.shape[0] - TQ
        sd = jnp.where(causal_t, st[n_main:, :], -jnp.inf)
        m = jnp.max(sd, axis=0, keepdims=True)
        if n_main:
            m = jnp.maximum(m, jnp.max(st[:n_main, :], axis=0, keepdims=True))
        r = _dot(vt_ref[pr, :, n_main:n_main + TQ], jnp.exp2(sd - m).astype(BF16))
        if n_main:
            r = r + _dot(vt_ref[pr, :, :n_main], jnp.exp2(st[:n_main, :] - m).astype(BF16))
        return r

    units = [(i, pr) for i in range(S // TQ) for pr in range(ATTN_PAIRS)]
    ahead = [scores_t(u) for u in units[:ATTN_LOOKAHEAD]]
    for pos, (i, pr) in enumerate(units):
        st1, st2 = ahead.pop(0)
        if pos + ATTN_LOOKAHEAD < len(units):
            ahead.append(scores_t(units[pos + ATTN_LOOKAHEAD]))
        r1 = weighted_values_t(pr, st1)
        r2 = weighted_values_t(pr, st2)
        ot = (r1[:DIFF_V_DIM] * (1.0 / r1[DIFF_V_DIM:DIFF_V_DIM + 1])
              - lam * (r2[:DIFF_V_DIM] * (1.0 / r2[DIFF_V_DIM:DIFF_V_DIM + 1])))
        z_ref[i * TQ:(i + 1) * TQ, pr * LANES:(pr + 1) * LANES] = (
            (_rms(ot.T) * sg_ref[...]) * (1.0 - lambda_init)).astype(BF16)


def _diff_attention(q, kv, lam, sg, *, lambda_init):
    B, H, S, _ = q.shape
    hp = H // ATTN_PAIRS
    slab = lambda off: pl.BlockSpec((None, ATTN_PAIRS, S, LANES), lambda b, p: (b, p + off, 0, 0))
    kern = functools.partial(_attn_kernel, lambda_init=lambda_init)
    return pl.pallas_call(
        kern,
        grid=(B, hp),
        in_specs=[
            pl.BlockSpec(lam.shape, lambda b, p: (0, 0)),
            slab(0), slab(0), slab(hp),
            pl.BlockSpec((1, DIFF_V_DIM), lambda b, p: (0, 0)),
        ],
        out_specs=pl.BlockSpec((S, ATTN_PAIRS * LANES), lambda b, p: (b, p)),
        out_shape=jax.ShapeDtypeStruct((B * S, H * LANES), BF16),
        scratch_shapes=[pltpu.VMEM((ATTN_PAIRS, DIFF_V_DIM + BF16_SUBLANES, S), BF16),
                        pltpu.VMEM((ATTN_PAIRS, 2 * DIFF_HEAD_DIM, S), BF16)],
        compiler_params=_cparams("parallel", "parallel"),
        name="diff_attention",
    )(lam, q, kv, kv, sg)


def _post_mlp_kernel(z_ref, wo_ref, x_ref, gaa_ref, g1_ref, g2_ref, scm_ref, shm_ref, w1_ref, w2_ref, gam_ref, g3_ref,
                     *rest, n_cast):
    cast_src, xo_ref, cast_dst, a_ref = rest[:n_cast], rest[n_cast], rest[n_cast + 1:2 * n_cast + 1], rest[-1]
    _Casts.run(cast_src, cast_dst)
    starts = [sum(SUBS_MLP[:r]) for r in range(len(SUBS_MLP))]
    halves = [slice(s0, s0 + n) for s0, n in zip(starts, SUBS_MLP)]
    ys = [_dot(z_ref[rows, :], wo_ref[...]) for rows in halves]
    hs = []
    for rows, y in zip(halves, ys):
        xn = x_ref[rows, :] + (1.0 + gaa_ref[...]) * (_rms(y) * g1_ref[...])
        xo_ref[rows, :] = xn
        hs.append(((_rms(xn) * g2_ref[...]) * (1.0 + scm_ref[...]) + shm_ref[...]).astype(BF16))
    for rows, h in zip(halves, hs):
        for f in range(a_ref.shape[1] // TF_MLP):
            cols = slice(f * TF_MLP, (f + 1) * TF_MLP)
            a = jnp.maximum(_dot(h, w1_ref[:, cols]), 0.0)
            a_ref[rows, cols] = (a * a).astype(BF16)
    y2s = [_dot(a_ref[rows, :], w2_ref[...]) for rows in halves]
    for rows, y2 in zip(halves, y2s):
        xo_ref[rows, :] = xo_ref[rows, :] + (1.0 + gam_ref[...]) * (_rms(y2) * g3_ref[...])


def _post_mlp(z, wo, x, gaa, g1, g2, scm, shm, w1, w2, gam, g3, cast_sources, *, seq):
    T, D = x.shape
    KZ = z.shape[1]
    F = w1.shape[1]
    tpb = seq // TM_MLP
    xspec = pl.BlockSpec((TM_MLP, D), lambda i: (i, 0))
    vec = pl.BlockSpec((1, D), lambda i: (0, 0))
    bvec = pl.BlockSpec((None, 1, D), lambda i: (i // tpb, 0, 0))
    casts = _Casts(cast_sources, T // TM_MLP, lambda i: i)
    return pl.pallas_call(
        functools.partial(_post_mlp_kernel, n_cast=casts.n),
        grid=(T // TM_MLP,),
        in_specs=[
            pl.BlockSpec((TM_MLP, KZ), lambda i: (i, 0)), _resident(wo), xspec,
            bvec, vec, vec, bvec, bvec,
            _resident(w1), _resident(w2), bvec, vec,
        ] + casts.in_specs,
        out_specs=[xspec] + casts.out_specs,
        out_shape=[jax.ShapeDtypeStruct((T, D), F32)] + casts.out_shapes,
        scratch_shapes=[pltpu.VMEM((TM_MLP, F), BF16)],
        compiler_params=_cparams("parallel"),
        name="post_mlp",
    )(z, wo, x, gaa, g1, g2, scm, shm, w1, w2, gam, g3, *casts.arrays)


def kernel(x, c, positions, norm_g, ada_w, ada_b, ret_w_in, ret_w_out, kv_norm_g, kv_ada_w, kv_ada_b, kv_w,
           diff_w_q, diff_w_o, diff_lam, diff_subln_g, mlp_w1, mlp_w2):
    B, S, D = x.shape
    T = B * S
    assert D == D_MODEL and norm_g.shape[0] == DEPTH
    assert S % TM_PROJ == 0 and S % TM_MLP == 0 and S % TQ == 0 and S % (RET_CHUNKS_PER_STEP * RET_CHUNK) == 0
    assert T % TM_TAB == 0 and sum(SUBS_MLP) == TM_MLP and sum(SUBS_PROJ) == TM_PROJ
    xf = x.reshape(T, D)

    gvec = lambda l, i: norm_g[l, i].reshape(1, D)
    n_b = DEPTH - N_A

    ret_cos, ret_sin, d_cos, d_sin, mod, kv_mod, w_in = _setup(
        positions, c, ada_w, ada_b, kv_ada_w, kv_ada_b, [(ret_w_in, 0)])
    w_kv = w_q = w_o = kv = None

    def mvec(l, i):
        return mod[l, :, i * D:(i + 1) * D].reshape(B, 1, D)

    for l in range(DEPTH):
        sh_a, sc_a, ga_a, sh_m, sc_m, ga_m = (mvec(l, i) for i in range(6))
        if l < N_A:
            if l + 1 < N_A:
                mixer_next = [(ret_w_in, l + 1)]
            else:
                mixer_next = [(kv_w[None], 0)] + [(diff_w_q, j) for j in range(n_b)] + [
                    (diff_w_o, j) for j in range(n_b)]
            z, wo, w1, w2, *nxt = _retention_layer(
                xf, gvec(l, 0), sc_a, sh_a, ret_cos, ret_sin, w_in,
                [(ret_w_out, l), (mlp_w1, l), (mlp_w2, l)] + mixer_next, batch=B, seq=S)
            if l + 1 < N_A:
                (w_in,) = nxt
            else:
                w_kv, w_q, w_o = nxt[0], nxt[1:1 + n_b], nxt[1 + n_b:]
        else:
            j = l - N_A
            if j == 0:
                kv = _proj(xf, kv_norm_g.reshape(1, D), kv_mod[:, D:].reshape(B, 1, D),
                           kv_mod[:, :D].reshape(B, 1, D), w_kv, d_cos, d_sin, rope_cols=D, scale=1.0, batch=B, seq=S)
            q = _proj(xf, gvec(l, 0), sc_a, sh_a, w_q[j], d_cos, d_sin,
                      rope_cols=D, scale=DIFF_HEAD_DIM ** -0.5 * math.log2(math.e), batch=B, seq=S)
            z = _diff_attention(q, kv, diff_lam[j], diff_subln_g[j].reshape(1, DIFF_V_DIM),
                                lambda_init=0.8 - 0.6 * math.exp(-0.3 * l))
            wo = w_o[j]
        mlp_next = [(mlp_w1, l + 1), (mlp_w2, l + 1)] if N_A <= l + 1 < DEPTH else []
        xf, *nxt = _post_mlp(z, wo, xf, ga_a, gvec(l, 1), gvec(l, 2), sc_m, sh_m, w1, w2, ga_m, gvec(l, 3),
                             mlp_next, seq=S)
        if nxt:
            w1, w2 = nxt
    return xf.reshape(B, S, D)
```

```python
import functools
import math

import jax
import jax.numpy as jnp
from jax import lax
from jax.experimental import pallas as pl
from jax.experimental.pallas import tpu as pltpu

D_MODEL = 1024
DEPTH = 4
N_A = DEPTH // 2
RET_HEADS = 4
RET_QK_DIM = D_MODEL // RET_HEADS
RET_V_DIM = 2 * RET_QK_DIM
RET_ROPE_BASE = 10000.0
DIFF_HEAD_DIM = 64
DIFF_V_DIM = 2 * DIFF_HEAD_DIM
ROPE_THETA = 500000.0
ROPE_DIM = DIFF_HEAD_DIM // 4
EPS = 1e-6

LANES = 128
BF16_SUBLANES = 16
VMEM_LIMIT = 56 * 1024 * 1024

TM_PROJ = 1024
SUBS_PROJ = (512, 512)
TN_PROJ = 512
TM_MLP = 512
SUBS_MLP = (256, 256)
TF_MLP = 512
RET_CHUNK = 256
RET_CHUNKS_PER_STEP = 2
TQ = 256
ATTN_PAIRS = 2
ATTN_LOOKAHEAD = 6
TM_TAB = 1024

F32 = jnp.float32
BF16 = jnp.bfloat16


def _cparams(*sem):
    return pltpu.CompilerParams(dimension_semantics=sem, vmem_limit_bytes=VMEM_LIMIT)


def _resident(w):
    return pl.BlockSpec(w.shape, lambda *_: (0, 0), pipeline_mode=pl.Buffered(1))


class _BatchVec:
    def __init__(self, array, before, after):
        self.array, self.before, self.after = array, tuple(before), tuple(after)

    def spec(self, batch_of):
        block = (None,) * (len(self.before) + 1 + len(self.after)) + self.array.shape[-2:]
        return pl.BlockSpec(block, lambda *g: self.before + (batch_of(*g),) + self.after + (0, 0))


class _Casts:
    def __init__(self, sources, steps, step_of):
        self.arrays = [stack for stack, _ in sources]
        self.in_specs, self.out_specs, self.out_shapes = [], [], []
        for stack, layer in sources:
            _, rows, cols = stack.shape
            blk = rows // steps
            self.in_specs.append(pl.BlockSpec((None, blk, cols), lambda *g, layer=layer: (layer, step_of(*g), 0)))
            self.out_specs.append(pl.BlockSpec((blk, cols), lambda *g: (step_of(*g), 0)))
            self.out_shapes.append(jax.ShapeDtypeStruct((rows, cols), BF16))
        self.n = len(sources)

    @staticmethod
    def run(src_refs, dst_refs):
        for src, dst in zip(src_refs, dst_refs):
            dst[...] = src[...].astype(BF16)


def _rms(x):
    return x * lax.rsqrt(jnp.mean(x * x, axis=-1, keepdims=True) + EPS)


def _dot(a, b):
    return jnp.dot(a, b, preferred_element_type=F32)


def _dot_nt(a, b):
    return lax.dot_general(a, b, (((1,), (1,)), ((), ())), preferred_element_type=F32)


def _dot_tn(a, b):
    return lax.dot_general(a, b, (((0,), (0,)), ((), ())), preferred_element_type=F32)


def _setup_kernel(pos_ref, invr_ref, invd_ref, c_ref, aw_ref, ab_ref, kw_ref, kb_ref, *rest, n_cast):
    cast_src, cast_dst = rest[:n_cast], rest[n_cast + 6:]
    rc_ref, rs_ref, dc_ref, ds_ref, mod_ref, kmod_ref = rest[n_cast:n_cast + 6]
    _Casts.run(cast_src, cast_dst)

    c = c_ref[...]
    c_act = (c * (1.0 / (1.0 + jnp.exp(-c)))).astype(BF16)
    for l in range(aw_ref.shape[0]):
        mod_ref[l] = _dot(c_act, aw_ref[l].astype(BF16)) + ab_ref[l]
    kmod_ref[...] = _dot(c_act, kw_ref[...].astype(BF16)) + kb_ref[...]

    p = jnp.broadcast_to(pos_ref[...].astype(F32), (LANES, pos_ref.shape[1])).T
    ang = p * invr_ref[...]
    rc_ref[...] = jnp.cos(ang)
    rs_ref[...] = jnp.sin(ang)
    angd = p * invd_ref[...]
    rotary = jnp.bitwise_and(lax.broadcasted_iota(jnp.int32, angd.shape, 1), DIFF_HEAD_DIM - 1) < ROPE_DIM
    dc_ref[...] = jnp.where(rotary, jnp.cos(angd), 1.0)
    ds_ref[...] = jnp.where(rotary, jnp.sin(angd), 0.0)


def _setup(positions, c, ada_w, ada_b, kv_ada_w, kv_ada_b, cast_sources):
    T = positions.size
    steps = T // TM_TAB
    L, D, N = ada_w.shape
    NK = kv_ada_w.shape[1]
    B = c.shape[0]
    casts = _Casts(cast_sources, steps, lambda i: i)
    inv_r = RET_ROPE_BASE ** (-jnp.arange(0, RET_QK_DIM, 2, dtype=F32) / RET_QK_DIM)
    inv_d = ROPE_THETA ** (-jnp.arange(0, ROPE_DIM, 2, dtype=F32) / ROPE_DIM)
    pat = jnp.concatenate([inv_d, inv_d, jnp.zeros((DIFF_HEAD_DIM - ROPE_DIM,), F32)])
    inv_d_lanes = jnp.tile(pat, LANES // DIFF_HEAD_DIM)
    tab = jax.ShapeDtypeStruct((T, LANES), F32)
    tspec = pl.BlockSpec((TM_TAB, LANES), lambda i: (i, 0))
    vspec = pl.BlockSpec((1, LANES), lambda i: (0, 0))
    return pl.pallas_call(
        functools.partial(_setup_kernel, n_cast=casts.n),
        grid=(steps,),
        in_specs=[
            pl.BlockSpec((None, 1, TM_TAB), lambda i: (i, 0, 0)), vspec, vspec,
            pl.BlockSpec((B, D), lambda i: (0, 0)),
            pl.BlockSpec((L, D, N // steps), lambda i: (0, 0, i)),
            pl.BlockSpec((L, 1, N // steps), lambda i: (0, 0, i)),
            pl.BlockSpec((D, NK // steps), lambda i: (0, i)),
            pl.BlockSpec((1, NK // steps), lambda i: (0, i)),
        ] + casts.in_specs,
        out_specs=[tspec] * 4 + [
            pl.BlockSpec((L, B, N // steps), lambda i: (0, 0, i)),
            pl.BlockSpec((B, NK // steps), lambda i: (0, i)),
        ] + casts.out_specs,
        out_shape=[tab] * 4 + [jax.ShapeDtypeStruct((L, B, N), F32), jax.ShapeDtypeStruct((B, NK), F32)]
        + casts.out_shapes,
        compiler_params=_cparams("parallel"),
        name="setup",
    )(positions.reshape(steps, 1, TM_TAB), inv_r.reshape(1, LANES), inv_d_lanes.reshape(1, LANES), c, ada_w,
      ada_b.reshape(L, 1, N), kv_ada_w, kv_ada_b.reshape(1, NK), *casts.arrays)


def _rotate_half_matrix():
    n = 2 * LANES
    half = ROPE_DIM // 2
    row = lax.broadcasted_iota(jnp.int32, (n, n), 0)
    col = lax.broadcasted_iota(jnp.int32, (n, n), 1)
    cm = jnp.bitwise_and(col, DIFF_HEAD_DIM - 1)
    return jnp.where((cm < half) & (row == col + half), -1.0,
                     jnp.where((cm >= half) & (cm < ROPE_DIM) & (row == col - half), 1.0, 0.0)).astype(BF16)


def _proj_kernel(x_ref, c_ref, s_ref, rot_ref, *rest, metas):
    groups = [rest[4 * k:4 * k + 4] for k in range(len(metas))]
    outs = rest[4 * len(metas):]
    pair = 2 * LANES
    halves = [slice(sum(SUBS_PROJ[:r]), sum(SUBS_PROJ[:r + 1])) for r in range(len(SUBS_PROJ))]
    normed = [_rms(x_ref[rows, :]) for rows in halves]
    hs = [[((nr * g[...]) * (1.0 + sc[...]) + sh[...]).astype(BF16) for g, sc, sh, _ in groups] for nr in normed]
    units = [(r, k, cc) for r in range(len(halves)) for k in range(len(groups))
             for cc in range(groups[k][3].shape[1] // TN_PROJ)]

    def main_dot(u):
        r, k, cc = units[u]
        return _dot(hs[r][k], groups[k][3][:, cc * TN_PROJ:(cc + 1) * TN_PROJ])

    acc_next = main_dot(0)
    for u, (r, k, cc) in enumerate(units):
        acc = acc_next
        if u + 1 < len(units):
            acc_next = main_dot(u + 1)
        rows = halves[r]
        rope_cols, scale = metas[k]
        o_ref = outs[k]
        for p in range(TN_PROJ // pair):
            xc = acc[:, p * pair:(p + 1) * pair]
            roped = cc * TN_PROJ + p * pair < rope_cols
            if roped:
                partner = _dot(xc.astype(BF16), rot_ref[...])
            for v in range(2):
                lanes = slice(v * LANES, (v + 1) * LANES)
                slab = (cc * TN_PROJ + p * pair) // LANES + v
                if roped:
                    o_ref[slab, rows, :] = ((xc[:, lanes] * c_ref[rows, :] + partner[:, lanes] * s_ref[rows, :])
                                            * scale).astype(BF16)
                else:
                    o_ref[slab, rows, :] = xc[:, lanes].astype(BF16)


def _proj(x, groups, c_tab, s_tab, *, batch, seq):
    T, D = x.shape
    tpb = seq // TM_PROJ
    vec = pl.BlockSpec((1, D), lambda i: (0, 0))
    batch_of = lambda i: i // tpb
    tspec = pl.BlockSpec((TM_PROJ, LANES), lambda i: (i, 0))
    in_specs = [pl.BlockSpec((TM_PROJ, D), lambda i: (i, 0)), tspec, tspec,
                pl.BlockSpec((2 * LANES, 2 * LANES), lambda i: (0, 0))]
    args = [x, c_tab, s_tab, _rotate_half_matrix()]
    out_specs, out_shapes = [], []
    for g, sc, sh, w, _, _ in groups:
        slabs = w.shape[1] // LANES
        in_specs += [vec, sc.spec(batch_of), sh.spec(batch_of), _resident(w)]
        args += [g, sc.array, sh.array, w]
        out_specs.append(pl.BlockSpec((None, slabs, TM_PROJ, LANES), lambda i: (i // tpb, 0, i % tpb, 0)))
        out_shapes.append(jax.ShapeDtypeStruct((batch, slabs, seq, LANES), BF16))
    kern = functools.partial(_proj_kernel, metas=tuple((rc, scale) for *_, rc, scale in groups))
    return pl.pallas_call(
        kern,
        grid=(T // TM_PROJ,),
        in_specs=in_specs,
        out_specs=out_specs,
        out_shape=out_shapes,
        compiler_params=_cparams("parallel"),
        name="proj_heads",
    )(*args)


def _ret_kernel(x_ref, g_ref, sc_ref, sh_ref, cos_ref, sin_ref, w_ref, *rest, n_cast):
    cast_src, z_ref, cast_dst = rest[:n_cast], rest[n_cast], rest[n_cast + 1:2 * n_cast + 1]
    r_ref, dm_ref, xi_ref, ze_ref = rest[2 * n_cast + 1:]
    _Casts.run(cast_src, cast_dst)
    C = RET_CHUNK
    b, n = pl.program_id(0), pl.program_id(1)
    log_g = [math.log1p(-(2.0 ** (-5 - h))) for h in range(RET_HEADS)]

    @pl.when(n == 0)
    def _():
        r_ref[...] = jnp.zeros_like(r_ref)

    @pl.when((b == 0) & (n == 0))
    def _():
        diff = (lax.broadcasted_iota(jnp.int32, (C, C), 0) - lax.broadcasted_iota(jnp.int32, (C, C), 1)).astype(F32)
        idx = lax.broadcasted_iota(jnp.int32, (C, RET_V_DIM), 0).astype(F32)
        for h in range(RET_HEADS):
            dm_ref[h] = jnp.where(diff >= 0, jnp.exp(jnp.maximum(diff, 0.0) * log_g[h]), 0.0)
            xi_ref[h] = jnp.exp((idx + 1.0) * log_g[h])
            ze_ref[h] = jnp.exp((C - 1.0 - idx) * log_g[h])

    for ci in range(RET_CHUNKS_PER_STEP):
        _ret_chunk(slice(ci * C, (ci + 1) * C), log_g, x_ref, g_ref, sc_ref, sh_ref, cos_ref, sin_ref, w_ref, z_ref,
                   r_ref, dm_ref, xi_ref, ze_ref)


def _ret_chunk(rows, log_g, x_ref, g_ref, sc_ref, sh_ref, cos_ref, sin_ref, w_ref, z_ref, r_ref, dm_ref, xi_ref, ze_ref):
    C = RET_CHUNK
    HK = RET_HEADS * RET_QK_DIM
    HV = RET_HEADS * RET_V_DIM
    hin = ((_rms(x_ref[rows, :]) * g_ref[...]) * (1.0 + sc_ref[...]) + sh_ref[...]).astype(BF16)
    cos = cos_ref[rows, :]
    sin = sin_ref[rows, :]

    def roped(col, mul):
        t = _dot(hin, w_ref[:, col:col + RET_QK_DIM])
        x1, x2 = t[:, :LANES], t[:, LANES:]
        return jnp.concatenate([(x1 * cos - x2 * sin) * mul, (x2 * cos + x1 * sin) * mul], axis=1).astype(BF16)

    for h in range(RET_HEADS):
        qh = roped(h * RET_QK_DIM, 1.0)
        kh = roped(HK + h * RET_QK_DIM, RET_QK_DIM ** -0.5)
        vh = _dot(hin, w_ref[:, 2 * HK + h * RET_V_DIM:2 * HK + (h + 1) * RET_V_DIM])
        gt = _dot(hin, w_ref[:, 2 * HK + HV + h * RET_V_DIM:2 * HK + HV + (h + 1) * RET_V_DIM])
        s = _dot_nt(qh, kh)
        rh = r_ref[h]
        cross = _dot(qh, rh.astype(BF16)) * xi_ref[h]
        r_ref[h] = rh * math.exp(C * log_g[h]) + _dot_tn(kh, (vh * ze_ref[h]).astype(BF16))
        intra = _dot((s * dm_ref[h]).astype(BF16), vh.astype(BF16))
        o = _rms(intra + cross)
        z_ref[rows, h * RET_V_DIM:(h + 1) * RET_V_DIM] = (gt * (1.0 / (1.0 + jnp.exp(-gt))) * o).astype(BF16)


def _retention_layer(x, g, sc, sh, cos, sin, w_in, cast_sources, *, batch, seq):
    T, D = x.shape
    C = RET_CHUNK
    rows = RET_CHUNKS_PER_STEP * C
    nch = seq // rows
    HV = RET_HEADS * RET_V_DIM
    row = lambda b, n: (b * nch + n, 0)
    vec = pl.BlockSpec((1, D), lambda b, n: (0, 0))
    batch_of = lambda b, n: b
    tspec = pl.BlockSpec((rows, LANES), row)
    casts = _Casts(cast_sources, batch * nch, lambda b, n: b * nch + n)
    return pl.pallas_call(
        functools.partial(_ret_kernel, n_cast=casts.n),
        grid=(batch, nch),
        in_specs=[pl.BlockSpec((rows, D), row), vec, sc.spec(batch_of), sh.spec(batch_of), tspec, tspec,
                  _resident(w_in)] + casts.in_specs,
        out_specs=[pl.BlockSpec((rows, HV), row)] + casts.out_specs,
        out_shape=[jax.ShapeDtypeStruct((T, HV), BF16)] + casts.out_shapes,
        scratch_shapes=[
            pltpu.VMEM((RET_HEADS, RET_QK_DIM, RET_V_DIM), F32),
            pltpu.VMEM((RET_HEADS, C, C), F32),
            pltpu.VMEM((RET_HEADS, C, RET_V_DIM), F32),
            pltpu.VMEM((RET_HEADS, C, RET_V_DIM), F32),
        ],
        compiler_params=_cparams("arbitrary", "arbitrary"),
        name="retention",
    )(x, g, sc.array, sh.array, cos, sin, w_in, *casts.arrays)


def _attn_kernel(lam_ref, q_ref, k_ref, v_ref, sg_ref, z_ref, vt_ref, *, lambda_init):
    S = q_ref.shape[1]
    lf = lam_ref[...]
    lam = (jnp.exp(jnp.sum(lf[0:1] * lf[1:2], axis=-1, keepdims=True))
           - jnp.exp(jnp.sum(lf[2:3] * lf[3:4], axis=-1, keepdims=True)) + lambda_init)
    lane = lax.broadcasted_iota(jnp.int32, (TQ, LANES), 1)
    causal_t = lax.broadcasted_iota(jnp.int32, (TQ, TQ), 0) <= lax.broadcasted_iota(jnp.int32, (TQ, TQ), 1)
    for pr in range(ATTN_PAIRS):
        vt_ref[pr, :DIFF_V_DIM, :] = v_ref[pr].T
        vt_ref[pr, DIFF_V_DIM:, :] = jnp.ones((vt_ref.shape[1] - DIFF_V_DIM, S), BF16)

    def scores_t(unit):
        i, pr = unit
        n_keys = (i + 1) * TQ
        qp = q_ref[pr, i * TQ:(i + 1) * TQ, :]
        zero = jnp.zeros_like(qp)
        keys = k_ref[pr, :n_keys, :]
        return (_dot_nt(keys, jnp.where(lane < DIFF_HEAD_DIM, qp, zero)),
                _dot_nt(keys, jnp.where(lane < DIFF_HEAD_DIM, zero, qp)))

    def weighted_values_t(pr, st):
        n_main = st.shape[0] - TQ
        sd = jnp.where(causal_t, st[n_main:, :], -jnp.inf)
        m = jnp.max(sd, axis=0, keepdims=True)
        if n_main:
            m = jnp.maximum(m, jnp.max(st[:n_main, :], axis=0, keepdims=True))
        r = _dot(vt_ref[pr, :, n_main:n_main + TQ], jnp.exp2(sd - m).astype(BF16))
        if n_main:
            r = r + _dot(vt_ref[pr, :, :n_main], jnp.exp2(st[:n_main, :] - m).astype(BF16))
        return r

    units = [(i, pr) for i in range(S // TQ) for pr in range(ATTN_PAIRS)]
    ahead = [scores_t(u) for u in units[:ATTN_LOOKAHEAD]]
    for pos, (i, pr) in enumerate(units):
        st1, st2 = ahead.pop(0)
        if pos + ATTN_LOOKAHEAD < len(units):
            ahead.append(scores_t(units[pos + ATTN_LOOKAHEAD]))
        r1 = weighted_values_t(pr, st1)
        r2 = weighted_values_t(pr, st2)
        ot = (r1[:DIFF_V_DIM] * (1.0 / r1[DIFF_V_DIM:DIFF_V_DIM + 1])
              - lam * (r2[:DIFF_V_DIM] * (1.0 / r2[DIFF_V_DIM:DIFF_V_DIM + 1])))
        z_ref[i * TQ:(i + 1) * TQ, pr * LANES:(pr + 1) * LANES] = (
            (_rms(ot.T) * sg_ref[...]) * (1.0 - lambda_init)).astype(BF16)


def _diff_attention(q, kv, lam, sg, *, lambda_init):
    B, H, S, _ = q.shape
    hp = H // ATTN_PAIRS
    slab = lambda off: pl.BlockSpec((None, ATTN_PAIRS, S, LANES), lambda b, p: (b, p + off, 0, 0))
    kern = functools.partial(_attn_kernel, lambda_init=lambda_init)
    return pl.pallas_call(
        kern,
        grid=(B, hp),
        in_specs=[
            pl.BlockSpec(lam.shape, lambda b, p: (0, 0)),
            slab(0), slab(0), slab(hp),
            pl.BlockSpec((1, DIFF_V_DIM), lambda b, p: (0, 0)),
        ],
        out_specs=pl.BlockSpec((S, ATTN_PAIRS * LANES), lambda b, p: (b, p)),
        out_shape=jax.ShapeDtypeStruct((B * S, H * LANES), BF16),
        scratch_shapes=[pltpu.VMEM((ATTN_PAIRS, DIFF_V_DIM + BF16_SUBLANES, S), BF16)],
        compiler_params=_cparams("parallel", "parallel"),
        name="diff_attention",
    )(lam, q, kv, kv, sg)


def _post_mlp_kernel(z_ref, wo_ref, x_ref, gaa_ref, g1_ref, g2_ref, scm_ref, shm_ref, w1_ref, w2_ref, gam_ref, g3_ref,
                     *rest, n_cast):
    cast_src, xo_ref, cast_dst, a_ref = rest[:n_cast], rest[n_cast], rest[n_cast + 1:2 * n_cast + 1], rest[-1]
    _Casts.run(cast_src, cast_dst)
    starts = [sum(SUBS_MLP[:r]) for r in range(len(SUBS_MLP))]
    halves = [slice(s0, s0 + n) for s0, n in zip(starts, SUBS_MLP)]
    ys = [_dot(z_ref[rows, :], wo_ref[...]) for rows in halves]
    hs = []
    for rows, y in zip(halves, ys):
        xn = x_ref[rows, :] + (1.0 + gaa_ref[...]) * (_rms(y) * g1_ref[...])
        xo_ref[rows, :] = xn
        hs.append(((_rms(xn) * g2_ref[...]) * (1.0 + scm_ref[...]) + shm_ref[...]).astype(BF16))
    for rows, h in zip(halves, hs):
        for f in range(a_ref.shape[1] // TF_MLP):
            cols = slice(f * TF_MLP, (f + 1) * TF_MLP)
            a = jnp.maximum(_dot(h, w1_ref[:, cols]), 0.0)
            a_ref[rows, cols] = (a * a).astype(BF16)
    y2s = [_dot(a_ref[rows, :], w2_ref[...]) for rows in halves]
    for rows, y2 in zip(halves, y2s):
        xo_ref[rows, :] = xo_ref[rows, :] + (1.0 + gam_ref[...]) * (_rms(y2) * g3_ref[...])


def _post_mlp(z, wo, x, gaa, g1, g2, scm, shm, w1, w2, gam, g3, cast_sources, *, seq):
    T, D = x.shape
    KZ = z.shape[1]
    F = w1.shape[1]
    tpb = seq // TM_MLP
    xspec = pl.BlockSpec((TM_MLP, D), lambda i: (i, 0))
    vec = pl.BlockSpec((1, D), lambda i: (0, 0))
    batch_of = lambda i: i // tpb
    casts = _Casts(cast_sources, T // TM_MLP, lambda i: i)
    return pl.pallas_call(
        functools.partial(_post_mlp_kernel, n_cast=casts.n),
        grid=(T // TM_MLP,),
        in_specs=[
            pl.BlockSpec((TM_MLP, KZ), lambda i: (i, 0)), _resident(wo), xspec,
            gaa.spec(batch_of), vec, vec, scm.spec(batch_of), shm.spec(batch_of),
            _resident(w1), _resident(w2), gam.spec(batch_of), vec,
        ] + casts.in_specs,
        out_specs=[xspec] + casts.out_specs,
        out_shape=[jax.ShapeDtypeStruct((T, D), F32)] + casts.out_shapes,
        scratch_shapes=[pltpu.VMEM((TM_MLP, F), BF16)],
        compiler_params=_cparams("parallel"),
        name="post_mlp",
    )(z, wo, x, gaa.array, g1, g2, scm.array, shm.array, w1, w2, gam.array, g3, *casts.arrays)


def kernel(x, c, positions, norm_g, ada_w, ada_b, ret_w_in, ret_w_out, kv_norm_g, kv_ada_w, kv_ada_b, kv_w,
           diff_w_q, diff_w_o, diff_lam, diff_subln_g, mlp_w1, mlp_w2):
    B, S, D = x.shape
    T = B * S
    assert D == D_MODEL and norm_g.shape[0] == DEPTH
    assert S % TM_PROJ == 0 and S % TM_MLP == 0 and S % TQ == 0 and S % (RET_CHUNKS_PER_STEP * RET_CHUNK) == 0
    assert T % TM_TAB == 0 and sum(SUBS_MLP) == TM_MLP and sum(SUBS_PROJ) == TM_PROJ
    xf = x.reshape(T, D)

    gvec = lambda l, i: norm_g[l, i].reshape(1, D)
    n_b = DEPTH - N_A

    ret_cos, ret_sin, d_cos, d_sin, mod, kv_mod, w_in = _setup(
        positions, c, ada_w, ada_b, kv_ada_w, kv_ada_b, [(ret_w_in, 0)])
    w_kv = w_q = w_o = kv = None

    mod = mod.reshape(DEPTH, B, 6, 1, D)
    kv_mod = kv_mod.reshape(B, 2, 1, D)

    def mvec(l, i):
        return _BatchVec(mod, (l,), (i,))

    for l in range(DEPTH):
        sh_a, sc_a, ga_a, sh_m, sc_m, ga_m = (mvec(l, i) for i in range(6))
        if l < N_A:
            if l + 1 < N_A:
                mixer_next = [(ret_w_in, l + 1)]
            else:
                mixer_next = [(kv_w[None], 0)] + [(diff_w_q, j) for j in range(n_b)] + [
                    (diff_w_o, j) for j in range(n_b)]
            z, wo, w1, w2, *nxt = _retention_layer(
                xf, gvec(l, 0), sc_a, sh_a, ret_cos, ret_sin, w_in,
                [(ret_w_out, l), (mlp_w1, l), (mlp_w2, l)] + mixer_next, batch=B, seq=S)
            if l + 1 < N_A:
                (w_in,) = nxt
            else:
                w_kv, w_q, w_o = nxt[0], nxt[1:1 + n_b], nxt[1 + n_b:]
        else:
            j = l - N_A
            q_group = (gvec(l, 0), sc_a, sh_a, w_q[j], D, DIFF_HEAD_DIM ** -0.5 * math.log2(math.e))
            if j == 0:
                kv_group = (kv_norm_g.reshape(1, D), _BatchVec(kv_mod, (), (1,)), _BatchVec(kv_mod, (), (0,)),
                            w_kv, D, 1.0)
                kv, q = _proj(xf, [kv_group, q_group], d_cos, d_sin, batch=B, seq=S)
            else:
                (q,) = _proj(xf, [q_group], d_cos, d_sin, batch=B, seq=S)
            z = _diff_attention(q, kv, diff_lam[j], diff_subln_g[j].reshape(1, DIFF_V_DIM),
                                lambda_init=0.8 - 0.6 * math.exp(-0.3 * l))
            wo = w_o[j]
        mlp_next = [(mlp_w1, l + 1), (mlp_w2, l + 1)] if N_A <= l + 1 < DEPTH else []
        xf, *nxt = _post_mlp(z, wo, xf, ga_a, gvec(l, 1), gvec(l, 2), sc_m, sh_m, w1, w2, ga_m, gvec(l, 3),
                             mlp_next, seq=S)
        if nxt:
            w1, w2 = nxt
    return xf.reshape(B, S, D)
```

```python
import functools
import math

import jax
import jax.numpy as jnp
from jax import lax
from jax.experimental import pallas as pl
from jax.experimental.pallas import tpu as pltpu

D_MODEL = 1024
DEPTH = 4
N_A = DEPTH // 2
RET_HEADS = 4
RET_QK_DIM = D_MODEL // RET_HEADS
RET_V_DIM = 2 * RET_QK_DIM
RET_ROPE_BASE = 10000.0
DIFF_HEAD_DIM = 64
DIFF_V_DIM = 2 * DIFF_HEAD_DIM
ROPE_THETA = 500000.0
ROPE_DIM = DIFF_HEAD_DIM // 4
EPS = 1e-6

LANES = 128
BF16_SUBLANES = 16
VMEM_LIMIT = 56 * 1024 * 1024

TM_PROJ = 1024
SUBS_PROJ = (512, 512)
TN_PROJ = 512
TM_MLP = 512
SUBS_MLP = (256, 256)
TF_MLP = 512
RET_CHUNK = 256
RET_CHUNKS_PER_STEP = 2
TQ = 256
ATTN_PAIRS = 2
ATTN_LOOKAHEAD = 6
TM_TAB = 1024

F32 = jnp.float32
BF16 = jnp.bfloat16


def _cparams(*sem):
    return pltpu.CompilerParams(dimension_semantics=sem, vmem_limit_bytes=VMEM_LIMIT)


def _resident(w):
    return pl.BlockSpec(w.shape, lambda *_: (0, 0), pipeline_mode=pl.Buffered(1))


class _BatchVec:
    def __init__(self, array, before, after):
        self.array, self.before, self.after = array, tuple(before), tuple(after)

    def spec(self, batch_of):
        block = (None,) * (len(self.before) + 1 + len(self.after)) + self.array.shape[-2:]
        return pl.BlockSpec(block, lambda *g: self.before + (batch_of(*g),) + self.after + (0, 0))


class _Casts:
    def __init__(self, sources, steps, step_of):
        self.arrays = [stack for stack, _ in sources]
        self.in_specs, self.out_specs, self.out_shapes = [], [], []
        for stack, layer in sources:
            _, rows, cols = stack.shape
            blk = rows // steps
            self.in_specs.append(pl.BlockSpec((None, blk, cols), lambda *g, layer=layer: (layer, step_of(*g), 0)))
            self.out_specs.append(pl.BlockSpec((blk, cols), lambda *g: (step_of(*g), 0)))
            self.out_shapes.append(jax.ShapeDtypeStruct((rows, cols), BF16))
        self.n = len(sources)

    @staticmethod
    def run(src_refs, dst_refs):
        for src, dst in zip(src_refs, dst_refs):
            dst[...] = src[...].astype(BF16)


def _rms(x):
    return x * lax.rsqrt(jnp.mean(x * x, axis=-1, keepdims=True) + EPS)


def _dot(a, b):
    return jnp.dot(a, b, preferred_element_type=F32)


def _dot_nt(a, b):
    return lax.dot_general(a, b, (((1,), (1,)), ((), ())), preferred_element_type=F32)


def _dot_tn(a, b):
    return lax.dot_general(a, b, (((0,), (0,)), ((), ())), preferred_element_type=F32)


def _setup_kernel(pos_ref, invr_ref, invd_ref, c_ref, aw_ref, ab_ref, kw_ref, kb_ref, *rest, n_cast):
    cast_src, cast_dst = rest[:n_cast], rest[n_cast + 6:]
    rc_ref, rs_ref, dc_ref, ds_ref, mod_ref, kmod_ref = rest[n_cast:n_cast + 6]
    _Casts.run(cast_src, cast_dst)

    c = c_ref[...]
    c_act = (c * (1.0 / (1.0 + jnp.exp(-c)))).astype(BF16)
    for l in range(aw_ref.shape[0]):
        mod_ref[l] = _dot(c_act, aw_ref[l].astype(BF16)) + ab_ref[l]
    kmod_ref[...] = _dot(c_act, kw_ref[...].astype(BF16)) + kb_ref[...]

    p = jnp.broadcast_to(pos_ref[...].astype(F32), (LANES, pos_ref.shape[1])).T
    ang = p * invr_ref[...]
    rc_ref[...] = jnp.cos(ang)
    rs_ref[...] = jnp.sin(ang)
    angd = p * invd_ref[...]
    rotary = jnp.bitwise_and(lax.broadcasted_iota(jnp.int32, angd.shape, 1), DIFF_HEAD_DIM - 1) < ROPE_DIM
    dc_ref[...] = jnp.where(rotary, jnp.cos(angd), 1.0)
    ds_ref[...] = jnp.where(rotary, jnp.sin(angd), 0.0)


def _setup(positions, c, ada_w, ada_b, kv_ada_w, kv_ada_b, cast_sources):
    T = positions.size
    steps = T // TM_TAB
    L, D, N = ada_w.shape
    NK = kv_ada_w.shape[1]
    B = c.shape[0]
    casts = _Casts(cast_sources, steps, lambda i: i)
    inv_r = RET_ROPE_BASE ** (-jnp.arange(0, RET_QK_DIM, 2, dtype=F32) / RET_QK_DIM)
    inv_d = ROPE_THETA ** (-jnp.arange(0, ROPE_DIM, 2, dtype=F32) / ROPE_DIM)
    pat = jnp.concatenate([inv_d, inv_d, jnp.zeros((DIFF_HEAD_DIM - ROPE_DIM,), F32)])
    inv_d_lanes = jnp.tile(pat, LANES // DIFF_HEAD_DIM)
    tab = jax.ShapeDtypeStruct((T, LANES), F32)
    tspec = pl.BlockSpec((TM_TAB, LANES), lambda i: (i, 0))
    vspec = pl.BlockSpec((1, LANES), lambda i: (0, 0))
    return pl.pallas_call(
        functools.partial(_setup_kernel, n_cast=casts.n),
        grid=(steps,),
        in_specs=[
            pl.BlockSpec((None, 1, TM_TAB), lambda i: (i, 0, 0)), vspec, vspec,
            pl.BlockSpec((B, D), lambda i: (0, 0)),
            pl.BlockSpec((L, D, N // steps), lambda i: (0, 0, i)),
            pl.BlockSpec((L, 1, N // steps), lambda i: (0, 0, i)),
            pl.BlockSpec((D, NK // steps), lambda i: (0, i)),
            pl.BlockSpec((1, NK // steps), lambda i: (0, i)),
        ] + casts.in_specs,
        out_specs=[tspec] * 4 + [
            pl.BlockSpec((L, B, N // steps), lambda i: (0, 0, i)),
            pl.BlockSpec((B, NK // steps), lambda i: (0, i)),
        ] + casts.out_specs,
        out_shape=[tab] * 4 + [jax.ShapeDtypeStruct((L, B, N), F32), jax.ShapeDtypeStruct((B, NK), F32)]
        + casts.out_shapes,
        compiler_params=_cparams("parallel"),
        name="setup",
    )(positions.reshape(steps, 1, TM_TAB), inv_r.reshape(1, LANES), inv_d_lanes.reshape(1, LANES), c, ada_w,
      ada_b.reshape(L, 1, N), kv_ada_w, kv_ada_b.reshape(1, NK), *casts.arrays)


def _rotate_half_matrix():
    n = 2 * LANES
    half = ROPE_DIM // 2
    row = lax.broadcasted_iota(jnp.int32, (n, n), 0)
    col = lax.broadcasted_iota(jnp.int32, (n, n), 1)
    cm = jnp.bitwise_and(col, DIFF_HEAD_DIM - 1)
    return jnp.where((cm < half) & (row == col + half), -1.0,
                     jnp.where((cm >= half) & (cm < ROPE_DIM) & (row == col - half), 1.0, 0.0)).astype(BF16)


def _proj_kernel(x_ref, c_ref, s_ref, rot_ref, *rest, metas):
    groups = [rest[4 * k:4 * k + 4] for k in range(len(metas))]
    outs = rest[4 * len(metas):]
    pair = 2 * LANES
    halves = [slice(sum(SUBS_PROJ[:r]), sum(SUBS_PROJ[:r + 1])) for r in range(len(SUBS_PROJ))]
    normed = [_rms(x_ref[rows, :]) for rows in halves]
    hs = [[((nr * g[...]) * (1.0 + sc[...]) + sh[...]).astype(BF16) for g, sc, sh, _ in groups] for nr in normed]
    units = [(r, k, cc) for r in range(len(halves)) for k in range(len(groups))
             for cc in range(groups[k][3].shape[1] // TN_PROJ)]

    def main_dot(u):
        r, k, cc = units[u]
        return _dot(hs[r][k], groups[k][3][:, cc * TN_PROJ:(cc + 1) * TN_PROJ])

    acc_next = main_dot(0)
    for u, (r, k, cc) in enumerate(units):
        acc = acc_next
        if u + 1 < len(units):
            acc_next = main_dot(u + 1)
        rows = halves[r]
        rope_cols, scale = metas[k]
        o_ref = outs[k]
        for p in range(TN_PROJ // pair):
            xc = acc[:, p * pair:(p + 1) * pair]
            roped = cc * TN_PROJ + p * pair < rope_cols
            if roped:
                partner = _dot(xc.astype(BF16), rot_ref[...])
            for v in range(2):
                lanes = slice(v * LANES, (v + 1) * LANES)
                slab = (cc * TN_PROJ + p * pair) // LANES + v
                if roped:
                    o_ref[slab, rows, :] = ((xc[:, lanes] * c_ref[rows, :] + partner[:, lanes] * s_ref[rows, :])
                                            * scale).astype(BF16)
                else:
                    o_ref[slab, rows, :] = xc[:, lanes].astype(BF16)


def _proj(x, groups, c_tab, s_tab, *, batch, seq):
    T, D = x.shape
    tpb = seq // TM_PROJ
    vec = pl.BlockSpec((1, D), lambda i: (0, 0))
    batch_of = lambda i: i // tpb
    tspec = pl.BlockSpec((TM_PROJ, LANES), lambda i: (i, 0))
    in_specs = [pl.BlockSpec((TM_PROJ, D), lambda i: (i, 0)), tspec, tspec,
                pl.BlockSpec((2 * LANES, 2 * LANES), lambda i: (0, 0))]
    args = [x, c_tab, s_tab, _rotate_half_matrix()]
    out_specs, out_shapes = [], []
    for g, sc, sh, w, _, _ in groups:
        slabs = w.shape[1] // LANES
        in_specs += [vec, sc.spec(batch_of), sh.spec(batch_of), _resident(w)]
        args += [g, sc.array, sh.array, w]
        out_specs.append(pl.BlockSpec((None, slabs, TM_PROJ, LANES), lambda i: (i // tpb, 0, i % tpb, 0)))
        out_shapes.append(jax.ShapeDtypeStruct((batch, slabs, seq, LANES), BF16))
    kern = functools.partial(_proj_kernel, metas=tuple((rc, scale) for *_, rc, scale in groups))
    return pl.pallas_call(
        kern,
        grid=(T // TM_PROJ,),
        in_specs=in_specs,
        out_specs=out_specs,
        out_shape=out_shapes,
        compiler_params=_cparams("parallel"),
        name="proj_heads",
    )(*args)


def _ret_kernel(x_ref, g_ref, sc_ref, sh_ref, cos_ref, sin_ref, w_ref, *rest, n_cast):
    cast_src, z_ref, cast_dst = rest[:n_cast], rest[n_cast], rest[n_cast + 1:2 * n_cast + 1]
    r_ref, dm_ref, xi_ref, ze_ref = rest[2 * n_cast + 1:]
    _Casts.run(cast_src, cast_dst)
    C = RET_CHUNK
    b, n = pl.program_id(0), pl.program_id(1)
    log_g = [math.log1p(-(2.0 ** (-5 - h))) for h in range(RET_HEADS)]

    @pl.when(n == 0)
    def _():
        r_ref[...] = jnp.zeros_like(r_ref)

    @pl.when((b == 0) & (n == 0))
    def _():
        diff = (lax.broadcasted_iota(jnp.int32, (C, C), 0) - lax.broadcasted_iota(jnp.int32, (C, C), 1)).astype(F32)
        idx = lax.broadcasted_iota(jnp.int32, (C, RET_V_DIM), 0).astype(F32)
        for h in range(RET_HEADS):
            dm_ref[h] = jnp.where(diff >= 0, jnp.exp(jnp.maximum(diff, 0.0) * log_g[h]), 0.0)
            xi_ref[h] = jnp.exp((idx + 1.0) * log_g[h])
            ze_ref[h] = jnp.exp((C - 1.0 - idx) * log_g[h])

    for ci in range(RET_CHUNKS_PER_STEP):
        _ret_chunk(slice(ci * C, (ci + 1) * C), log_g, x_ref, g_ref, sc_ref, sh_ref, cos_ref, sin_ref, w_ref, z_ref,
                   r_ref, dm_ref, xi_ref, ze_ref)


def _ret_chunk(rows, log_g, x_ref, g_ref, sc_ref, sh_ref, cos_ref, sin_ref, w_ref, z_ref, r_ref, dm_ref, xi_ref, ze_ref):
    C = RET_CHUNK
    HK = RET_HEADS * RET_QK_DIM
    HV = RET_HEADS * RET_V_DIM
    hin = ((_rms(x_ref[rows, :]) * g_ref[...]) * (1.0 + sc_ref[...]) + sh_ref[...]).astype(BF16)
    cos = cos_ref[rows, :]
    sin = sin_ref[rows, :]

    def roped(col, mul):
        t = _dot(hin, w_ref[:, col:col + RET_QK_DIM])
        x1, x2 = t[:, :LANES], t[:, LANES:]
        return jnp.concatenate([(x1 * cos - x2 * sin) * mul, (x2 * cos + x1 * sin) * mul], axis=1).astype(BF16)

    for h in range(RET_HEADS):
        qh = roped(h * RET_QK_DIM, 1.0)
        kh = roped(HK + h * RET_QK_DIM, RET_QK_DIM ** -0.5)
        vh = _dot(hin, w_ref[:, 2 * HK + h * RET_V_DIM:2 * HK + (h + 1) * RET_V_DIM])
        gt = _dot(hin, w_ref[:, 2 * HK + HV + h * RET_V_DIM:2 * HK + HV + (h + 1) * RET_V_DIM])
        s = _dot_nt(qh, kh)
        rh = r_ref[h]
        cross = _dot(qh, rh.astype(BF16)) * xi_ref[h]
        r_ref[h] = rh * math.exp(C * log_g[h]) + _dot_tn(kh, (vh * ze_ref[h]).astype(BF16))
        intra = _dot((s * dm_ref[h]).astype(BF16), vh.astype(BF16))
        o = _rms(intra + cross)
        z_ref[rows, h * RET_V_DIM:(h + 1) * RET_V_DIM] = (gt * (1.0 / (1.0 + jnp.exp(-gt))) * o).astype(BF16)


def _retention_layer(x, g, sc, sh, cos, sin, w_in, cast_sources, *, batch, seq):
    T, D = x.shape
    C = RET_CHUNK
    rows = RET_CHUNKS_PER_STEP * C
    nch = seq // rows
    HV = RET_HEADS * RET_V_DIM
    row = lambda b, n: (b * nch + n, 0)
    vec = pl.BlockSpec((1, D), lambda b, n: (0, 0))
    batch_of = lambda b, n: b
    tspec = pl.BlockSpec((rows, LANES), row)
    casts = _Casts(cast_sources, batch * nch, lambda b, n: b * nch + n)
    return pl.pallas_call(
        functools.partial(_ret_kernel, n_cast=casts.n),
        grid=(batch, nch),
        in_specs=[pl.BlockSpec((rows, D), row), vec, sc.spec(batch_of), sh.spec(batch_of), tspec, tspec,
                  _resident(w_in)] + casts.in_specs,
        out_specs=[pl.BlockSpec((rows, HV), row)] + casts.out_specs,
        out_shape=[jax.ShapeDtypeStruct((T, HV), BF16)] + casts.out_shapes,
        scratch_shapes=[
            pltpu.VMEM((RET_HEADS, RET_QK_DIM, RET_V_DIM), F32),
            pltpu.VMEM((RET_HEADS, C, C), F32),
            pltpu.VMEM((RET_HEADS, C, RET_V_DIM), F32),
            pltpu.VMEM((RET_HEADS, C, RET_V_DIM), F32),
        ],
        compiler_params=_cparams("arbitrary", "arbitrary"),
        name="retention",
    )(x, g, sc.array, sh.array, cos, sin, w_in, *casts.arrays)


def _attn_kernel(lam_ref, q_ref, k_ref, v_ref, sg_ref, z_ref, vt_ref, *, lambda_init):
    S = q_ref.shape[1]
    lf = lam_ref[...]
    lam = (jnp.exp(jnp.sum(lf[0:1] * lf[1:2], axis=-1, keepdims=True))
           - jnp.exp(jnp.sum(lf[2:3] * lf[3:4], axis=-1, keepdims=True)) + lambda_init)
    lane = lax.broadcasted_iota(jnp.int32, (TQ, LANES), 1)
    causal_t = lax.broadcasted_iota(jnp.int32, (TQ, TQ), 0) <= lax.broadcasted_iota(jnp.int32, (TQ, TQ), 1)
    def fill_vt(pr):
        vt_ref[pr, :DIFF_V_DIM, :] = v_ref[pr].T
        vt_ref[pr, DIFF_V_DIM:, :] = jnp.ones((vt_ref.shape[1] - DIFF_V_DIM, S), BF16)

    def scores_t(unit):
        i, pr = unit
        n_keys = (i + 1) * TQ
        qp = q_ref[pr, i * TQ:(i + 1) * TQ, :]
        zero = jnp.zeros_like(qp)
        keys = k_ref[pr, :n_keys, :]
        return (_dot_nt(keys, jnp.where(lane < DIFF_HEAD_DIM, qp, zero)),
                _dot_nt(keys, jnp.where(lane < DIFF_HEAD_DIM, zero, qp)))

    def weighted_values_t(pr, st):
        n_main = st.shape[0] - TQ
        sd = jnp.where(causal_t, st[n_main:, :], -jnp.inf)
        m = jnp.max(sd, axis=0, keepdims=True)
        if n_main:
            m = jnp.maximum(m, jnp.max(st[:n_main, :], axis=0, keepdims=True))
        r = _dot(vt_ref[pr, :, n_main:n_main + TQ], jnp.exp2(sd - m).astype(BF16))
        if n_main:
            r = r + _dot(vt_ref[pr, :, :n_main], jnp.exp2(st[:n_main, :] - m).astype(BF16))
        return r

    units = [(i, pr) for i in range(S // TQ) for pr in range(ATTN_PAIRS)]
    ahead = [scores_t(u) for u in units[:ATTN_LOOKAHEAD]]
    for pr in range(ATTN_PAIRS):
        fill_vt(pr)
    for pos, (i, pr) in enumerate(units):
        st1, st2 = ahead.pop(0)
        if pos + ATTN_LOOKAHEAD < len(units):
            ahead.append(scores_t(units[pos + ATTN_LOOKAHEAD]))
        r1 = weighted_values_t(pr, st1)
        r2 = weighted_values_t(pr, st2)
        ot = (r1[:DIFF_V_DIM] * (1.0 / r1[DIFF_V_DIM:DIFF_V_DIM + 1])
              - lam * (r2[:DIFF_V_DIM] * (1.0 / r2[DIFF_V_DIM:DIFF_V_DIM + 1])))
        z_ref[i * TQ:(i + 1) * TQ, pr * LANES:(pr + 1) * LANES] = (
            (_rms(ot.T) * sg_ref[...]) * (1.0 - lambda_init)).astype(BF16)


def _diff_attention(q, kv, lam, sg, *, lambda_init):
    B, H, S, _ = q.shape
    hp = H // ATTN_PAIRS
    slab = lambda off: pl.BlockSpec((None, ATTN_PAIRS, S, LANES), lambda b, p: (b, p + off, 0, 0))
    kern = functools.partial(_attn_kernel, lambda_init=lambda_init)
    return pl.pallas_call(
        kern,
        grid=(B, hp),
        in_specs=[
            pl.BlockSpec(lam.shape, lambda b, p: (0, 0)),
            slab(0), slab(0), slab(hp),
            pl.BlockSpec((1, DIFF_V_DIM), lambda b, p: (0, 0)),
        ],
        out_specs=pl.BlockSpec((S, ATTN_PAIRS * LANES), lambda b, p: (b, p)),
        out_shape=jax.ShapeDtypeStruct((B * S, H * LANES), BF16),
        scratch_shapes=[pltpu.VMEM((ATTN_PAIRS, DIFF_V_DIM + BF16_SUBLANES, S), BF16)],
        compiler_params=_cparams("parallel", "parallel"),
        name="diff_attention",
    )(lam, q, kv, kv, sg)


def _post_mlp_kernel(z_ref, wo_ref, x_ref, gaa_ref, g1_ref, g2_ref, scm_ref, shm_ref, w1_ref, w2_ref, gam_ref, g3_ref,
                     *rest, n_cast):
    cast_src, xo_ref, cast_dst, a_ref = rest[:n_cast], rest[n_cast], rest[n_cast + 1:2 * n_cast + 1], rest[-1]
    _Casts.run(cast_src, cast_dst)
    starts = [sum(SUBS_MLP[:r]) for r in range(len(SUBS_MLP))]
    halves = [slice(s0, s0 + n) for s0, n in zip(starts, SUBS_MLP)]
    ys = [_dot(z_ref[rows, :], wo_ref[...]) for rows in halves]
    hs = []
    for rows, y in zip(halves, ys):
        xn = x_ref[rows, :] + (1.0 + gaa_ref[...]) * (_rms(y) * g1_ref[...])
        xo_ref[rows, :] = xn
        hs.append(((_rms(xn) * g2_ref[...]) * (1.0 + scm_ref[...]) + shm_ref[...]).astype(BF16))
    for rows, h in zip(halves, hs):
        for f in range(a_ref.shape[1] // TF_MLP):
            cols = slice(f * TF_MLP, (f + 1) * TF_MLP)
            a = jnp.maximum(_dot(h, w1_ref[:, cols]), 0.0)
            a_ref[rows, cols] = (a * a).astype(BF16)
    y2s = [_dot(a_ref[rows, :], w2_ref[...]) for rows in halves]
    for rows, y2 in zip(halves, y2s):
        xo_ref[rows, :] = xo_ref[rows, :] + (1.0 + gam_ref[...]) * (_rms(y2) * g3_ref[...])


def _post_mlp(z, wo, x, gaa, g1, g2, scm, shm, w1, w2, gam, g3, cast_sources, *, seq):
    T, D = x.shape
    KZ = z.shape[1]
    F = w1.shape[1]
    tpb = seq // TM_MLP
    xspec = pl.BlockSpec((TM_MLP, D), lambda i: (i, 0))
    vec = pl.BlockSpec((1, D), lambda i: (0, 0))
    batch_of = lambda i: i // tpb
    casts = _Casts(cast_sources, T // TM_MLP, lambda i: i)
    return pl.pallas_call(
        functools.partial(_post_mlp_kernel, n_cast=casts.n),
        grid=(T // TM_MLP,),
        in_specs=[
            pl.BlockSpec((TM_MLP, KZ), lambda i: (i, 0)), _resident(wo), xspec,
            gaa.spec(batch_of), vec, vec, scm.spec(batch_of), shm.spec(batch_of),
            _resident(w1), _resident(w2), gam.spec(batch_of), vec,
        ] + casts.in_specs,
        out_specs=[xspec] + casts.out_specs,
        out_shape=[jax.ShapeDtypeStruct((T, D), F32)] + casts.out_shapes,
        scratch_shapes=[pltpu.VMEM((TM_MLP, F), BF16)],
        compiler_params=_cparams("parallel"),
        name="post_mlp",
    )(z, wo, x, gaa.array, g1, g2, scm.array, shm.array, w1, w2, gam.array, g3, *casts.arrays)


def kernel(x, c, positions, norm_g, ada_w, ada_b, ret_w_in, ret_w_out, kv_norm_g, kv_ada_w, kv_ada_b, kv_w,
           diff_w_q, diff_w_o, diff_lam, diff_subln_g, mlp_w1, mlp_w2):
    B, S, D = x.shape
    T = B * S
    assert D == D_MODEL and norm_g.shape[0] == DEPTH
    assert S % TM_PROJ == 0 and S % TM_MLP == 0 and S % TQ == 0 and S % (RET_CHUNKS_PER_STEP * RET_CHUNK) == 0
    assert T % TM_TAB == 0 and sum(SUBS_MLP) == TM_MLP and sum(SUBS_PROJ) == TM_PROJ
    xf = x.reshape(T, D)

    gvec = lambda l, i: norm_g[l, i].reshape(1, D)
    n_b = DEPTH - N_A

    ret_cos, ret_sin, d_cos, d_sin, mod, kv_mod, w_in = _setup(
        positions, c, ada_w, ada_b, kv_ada_w, kv_ada_b, [(ret_w_in, 0)])
    w_kv = w_q = w_o = kv = None

    mod = mod.reshape(DEPTH, B, 6, 1, D)
    kv_mod = kv_mod.reshape(B, 2, 1, D)

    def mvec(l, i):
        return _BatchVec(mod, (l,), (i,))

    for l in range(DEPTH):
        sh_a, sc_a, ga_a, sh_m, sc_m, ga_m = (mvec(l, i) for i in range(6))
        if l < N_A:
            if l + 1 < N_A:
                mixer_next = [(ret_w_in, l + 1)]
            else:
                mixer_next = [(kv_w[None], 0)] + [(diff_w_q, j) for j in range(n_b)] + [
                    (diff_w_o, j) for j in range(n_b)]
            z, wo, w1, w2, *nxt = _retention_layer(
                xf, gvec(l, 0), sc_a, sh_a, ret_cos, ret_sin, w_in,
                [(ret_w_out, l), (mlp_w1, l), (mlp_w2, l)] + mixer_next, batch=B, seq=S)
            if l + 1 < N_A:
                (w_in,) = nxt
            else:
                w_kv, w_q, w_o = nxt[0], nxt[1:1 + n_b], nxt[1 + n_b:]
        else:
            j = l - N_A
            q_group = (gvec(l, 0), sc_a, sh_a, w_q[j], D, DIFF_HEAD_DIM ** -0.5 * math.log2(math.e))
            if j == 0:
                kv_group = (kv_norm_g.reshape(1, D), _BatchVec(kv_mod, (), (1,)), _BatchVec(kv_mod, (), (0,)),
                            w_kv, D, 1.0)
                kv, q = _proj(xf, [kv_group, q_group], d_cos, d_sin, batch=B, seq=S)
            else:
                (q,) = _proj(xf, [q_group], d_cos, d_sin, batch=B, seq=S)
            z = _diff_attention(q, kv, diff_lam[j], diff_subln_g[j].reshape(1, DIFF_V_DIM),
                                lambda_init=0.8 - 0.6 * math.exp(-0.3 * l))
            wo = w_o[j]
        mlp_next = [(mlp_w1, l + 1), (mlp_w2, l + 1)] if N_A <= l + 1 < DEPTH else []
        xf, *nxt = _post_mlp(z, wo, xf, ga_a, gvec(l, 1), gvec(l, 2), sc_m, sh_m, w1, w2, ga_m, gvec(l, 3),
                             mlp_next, seq=S)
        if nxt:
            w1, w2 = nxt
    return xf.reshape(B, S, D)
```
